```python
import jax
import jax.numpy as jnp
from jax import lax
import numpy as np

D_MODEL = 2048
BATCH = 4
SEQ = 2048
DEPTH = 4
DEC_BATCH = 8
DEC_SEQ = 1
PAST_LEN = 16384
PAGE_SIZE = 128

N_MIXERS = 4
EPS = 1e-6
EXPAND = 2
D_INNER = EXPAND * D_MODEL

SSD_HEAD_DIM = 64
SSD_N_HEADS = D_INNER // SSD_HEAD_DIM
SSD_N_GROUPS = 8
SSD_D_STATE = 128
SSD_D_CONV = 4
SSD_CHUNK = 128
SSD_CONV_DIM = D_INNER + 2 * SSD_N_GROUPS * SSD_D_STATE
SSD_IN_DIM = D_INNER + SSD_CONV_DIM + SSD_N_HEADS

POOL_WINDOWS = (2, 4, 8, 16)
POOL_N_GROUPS = len(POOL_WINDOWS)
POOL_GROUP_DIM = D_INNER // POOL_N_GROUPS
POOL_HIST = max(POOL_WINDOWS) - 1

FOX_HEAD_DIM = 128
FOX_N_HEADS = D_MODEL // FOX_HEAD_DIM
FOX_WIDTH = FOX_N_HEADS * FOX_HEAD_DIM
FOX_Q_BLOCK = 128
FOX_IN_DIM = 4 * FOX_WIDTH + FOX_N_HEADS
FOX_FORGET_BIAS = 6.0
FOX_FORGET_NOISE = 0.5

SGU_CHUNK = 128
SGU_N_GROUPS = 8
SGU_GROUP_DIM = D_INNER // SGU_N_GROUPS

N_SSD_LAYERS = len(range(0, DEPTH, N_MIXERS))
N_POOL_LAYERS = len(range(1, DEPTH, N_MIXERS))
N_FOX_LAYERS = len(range(2, DEPTH, N_MIXERS))
N_SGU_LAYERS = len(range(3, DEPTH, N_MIXERS))

kernel_name = 'hybrid_ssd_pool_fox_sgu_step'


def rms_norm(x, w):
    xf = x.astype(jnp.float32)
    y = xf * lax.rsqrt(jnp.mean(xf * xf, axis=-1, keepdims=True) + EPS)
    return (y * w.astype(jnp.float32)).astype(x.dtype)


def causal_dwconv(x_ext, w, b):
    K = w.shape[0]
    S = x_ext.shape[1] - (K - 1)
    out = x_ext[:, 0:S] * w[0]
    for k in range(1, K):
        out = out + x_ext[:, k:k + S] * w[k]
    return out + b


def ssd_chunked_scan(x, dt, A, Bm, Cm, h0):
    bsz, S, H, P = x.shape
    G, N = Bm.shape[2], Bm.shape[3]
    Hg = H // G
    L = min(SSD_CHUNK, S)
    n = -(-S // L)
    pad = n * L - S

    def chunks(a):
        a = jnp.pad(a, [(0, 0), (0, pad)] + [(0, 0)] * (a.ndim - 2))
        return jnp.moveaxis(a.reshape((bsz, n, L) + a.shape[2:]), 1, 0)

    xs = chunks(x.reshape(bsz, S, G, Hg, P))
    dts = chunks(dt.reshape(bsz, S, G, Hg))
    Bs, Cs = chunks(Bm), chunks(Cm)
    Ag = A.reshape(G, Hg)
    mask = jnp.tril(jnp.ones((L, L), dtype=bool))[None, :, :, None, None]

    def step(h, inp):
        xc, dtc, Bc, Cc = inp
        cum = jnp.cumsum(dtc * Ag, axis=1)
        seg = cum[:, :, None] - cum[:, None, :]
        decay = jnp.exp(jnp.where(mask, seg, -jnp.inf))
        cb = jnp.einsum('btgn,bsgn->btsg', Cc, Bc)
        w = cb[..., None] * decay * dtc[:, None]
        y = jnp.einsum('btsgh,bsghp->btghp', w, xc)
        y = y + jnp.einsum('btgn,bghpn->btghp', Cc, h) * jnp.exp(cum)[..., None]
        tail = jnp.exp(cum[:, -1:] - cum) * dtc
        h = h * jnp.exp(cum[:, -1])[..., None, None] + jnp.einsum('bsgh,bsghp,bsgn->bghpn', tail, xc, Bc)
        return h, y

    hT, ys = lax.scan(step, h0.reshape(bsz, G, Hg, P, N), (xs, dts, Bs, Cs))
    y = jnp.moveaxis(ys, 0, 1).reshape(bsz, n * L, H, P)[:, :S]
    return y, hT.reshape(bsz, H, P, N)


def ssd_mixer(h, conv_buf, ssm_state, w_in, conv_w, conv_b, dt_bias, A_log, D_skip, norm_w, w_out):
    bsz, S, _ = h.shape
    f32 = jnp.float32
    z, xbc, dt_raw = jnp.split(h @ w_in, [D_INNER, D_INNER + SSD_CONV_DIM], axis=-1)
    ext = jnp.concatenate([conv_buf.astype(xbc.dtype), xbc], axis=1)
    new_buf = ext[:, -(SSD_D_CONV - 1):]
    xbc = jax.nn.silu(causal_dwconv(ext, conv_w, conv_b))
    xs, Bm, Cm = jnp.split(xbc, [D_INNER, D_INNER + SSD_N_GROUPS * SSD_D_STATE], axis=-1)
    xh = xs.reshape(bsz, S, SSD_N_HEADS, SSD_HEAD_DIM).astype(f32)
    dt = jax.nn.softplus(dt_raw.astype(f32) + dt_bias.astype(f32))
    A = -jnp.exp(A_log.astype(f32))
    y, h_new = ssd_chunked_scan(
        xh, dt, A,
        Bm.reshape(bsz, S, SSD_N_GROUPS, SSD_D_STATE).astype(f32),
        Cm.reshape(bsz, S, SSD_N_GROUPS, SSD_D_STATE).astype(f32),
        ssm_state.astype(f32))
    y = (y + D_skip.astype(f32)[:, None] * xh).reshape(bsz, S, D_INNER).astype(h.dtype)
    y = rms_norm(y * jax.nn.silu(z), norm_w)
    return y @ w_out, new_buf, h_new


def pool_mixer(h, hist, pos0, w_in, w_grp, scale, w_out):
    bsz, S, _ = h.shape
    p, z = jnp.split(h @ w_in, 2, axis=-1)
    ext = jnp.concatenate([hist.astype(p.dtype), p], axis=1)
    new_hist = ext[:, -POOL_HIST:]
    cs = jnp.pad(jnp.cumsum(ext.astype(jnp.float32), axis=1), ((0, 0), (1, 0), (0, 0)))
    pos = pos0 + jnp.arange(S)
    means = []
    for g, w in enumerate(POOL_WINDOWS):
        ch = slice(g * POOL_GROUP_DIM, (g + 1) * POOL_GROUP_DIM)
        win_sum = cs[:, POOL_HIST + 1:POOL_HIST + 1 + S, ch] - cs[:, POOL_HIST + 1 - w:POOL_HIST + 1 - w + S, ch]
        count = jnp.minimum(pos + 1, w).astype(jnp.float32)[None, :, None]
        means.append(win_sum / count)
    mean = jnp.stack(means, axis=2)
    diff = (mean - p.reshape(bsz, S, POOL_N_GROUPS, POOL_GROUP_DIM).astype(jnp.float32)).astype(p.dtype)
    mixed = jnp.einsum('bsgc,gcd->bsgd', diff, w_grp).reshape(bsz, S, D_INNER) * scale
    return (mixed * jax.nn.silu(z)) @ w_out, new_hist


def fox_project(h, w_in, b_f, q_norm, k_norm):
    bsz, S, _ = h.shape
    q, k, v, z, f_logit = jnp.split(h @ w_in, [FOX_WIDTH, 2 * FOX_WIDTH, 3 * FOX_WIDTH, 4 * FOX_WIDTH], axis=-1)
    q = rms_norm(q.reshape(bsz, S, FOX_N_HEADS, FOX_HEAD_DIM), q_norm)
    k = rms_norm(k.reshape(bsz, S, FOX_N_HEADS, FOX_HEAD_DIM), k_norm)
    v = v.reshape(bsz, S, FOX_N_HEADS, FOX_HEAD_DIM)
    logf = jax.nn.log_sigmoid(f_logit.astype(jnp.float32) + b_f.astype(jnp.float32))
    return q, k, v, z, logf


def fox_attend_prompt(q, k, v, logf):
    bsz, S, H, Dh = q.shape
    F = jnp.cumsum(logf, axis=1)
    Fk = jnp.transpose(F, (0, 2, 1))[:, :, None, :]
    nb = S // FOX_Q_BLOCK
    qb = jnp.moveaxis(q.reshape(bsz, nb, FOX_Q_BLOCK, H, Dh), 1, 0)
    Fb = jnp.moveaxis(F.reshape(bsz, nb, FOX_Q_BLOCK, H), 1, 0)
    kpos = jnp.arange(S)
    scale = Dh ** -0.5

    def block(args):
        qi, Fi, i = args
        s = jnp.einsum('bqhd,bkhd->bhqk', qi, k, preferred_element_type=jnp.float32) * scale
        s = s + jnp.transpose(Fi, (0, 2, 1))[..., None] - Fk
        qpos = i * FOX_Q_BLOCK + jnp.arange(FOX_Q_BLOCK)
        s = jnp.where(kpos[None, :] <= qpos[:, None], s, -jnp.inf)
        p = jax.nn.softmax(s, axis=-1)
        return jnp.einsum('bhqk,bkhd->bqhd', p.astype(v.dtype), v)

    out = lax.map(block, (qb, Fb, jnp.arange(nb)))
    return jnp.moveaxis(out, 0, 1).reshape(bsz, S, H, Dh)


def fox_attend_cached(q, k, v, logf, k_pages, v_pages, logf_pages, page_table, layer):
    bsz, S, H, Dh = q.shape
    kp = k_pages[layer, page_table].reshape(bsz, -1, H, Dh)
    vp = v_pages[layer, page_table].reshape(bsz, -1, H, Dh)
    lp = logf_pages[layer, page_table].reshape(bsz, -1, H).astype(jnp.float32)
    n_past = kp.shape[1]
    rc = lax.cumsum(lp, axis=1, reverse=True)
    past_tail = jnp.concatenate([rc[:, 1:], jnp.zeros_like(rc[:, :1])], axis=1)
    Fn = jnp.transpose(jnp.cumsum(logf, axis=1), (0, 2, 1))
    scale = Dh ** -0.5
    s_past = jnp.einsum('bqhd,bkhd->bhqk', q, kp, preferred_element_type=jnp.float32) * scale
    s_past = s_past + Fn[..., None] + jnp.transpose(past_tail, (0, 2, 1))[:, :, None, :]
    s_new = jnp.einsum('bqhd,bkhd->bhqk', q, k, preferred_element_type=jnp.float32) * scale
    s_new = s_new + Fn[..., :, None] - Fn[..., None, :]
    s_new = jnp.where(jnp.tril(jnp.ones((S, S), dtype=bool)), s_new, -jnp.inf)
    p = jax.nn.softmax(jnp.concatenate([s_past, s_new], axis=-1), axis=-1)
    out = jnp.einsum('bhqk,bkhd->bqhd', p[..., :n_past].astype(vp.dtype), vp)
    out = out + jnp.einsum('bhqk,bkhd->bqhd', p[..., n_past:].astype(v.dtype), v)
    return out.astype(q.dtype)


def fox_output(a, z, w_out):
    bsz, S = a.shape[0], a.shape[1]
    return (a.reshape(bsz, S, FOX_WIDTH) * jax.nn.silu(z)) @ w_out


def sgu_mixer(h, w_in, v_norm, w_s, b_s, w_out):
    u, v, z = jnp.split(h @ w_in, 3, axis=-1)
    v = rms_norm(v, v_norm)
    bsz, S, _ = v.shape
    L = min(SGU_CHUNK, S)
    n = -(-S // L)
    pad = n * L - S
    vc = jnp.pad(v, ((0, 0), (0, pad), (0, 0))).reshape(bsz, n, L, SGU_N_GROUPS, SGU_GROUP_DIM)
    W = jnp.tril(w_s[:, :L, :L])
    mixed = jnp.einsum('gts,bnsgc->bntgc', W, vc) + jnp.transpose(b_s[:, :L])[None, None, :, :, None]
    mixed = mixed.reshape(bsz, n * L, D_INNER)[:, :S]
    start = ((S - 1) // SGU_CHUNK) * SGU_CHUNK
    return (u * mixed * jax.nn.silu(z)) @ w_out, v[:, start:]


def setup_inputs(seed: int = 0) -> dict:
    key = jax.random.key(seed)
    ks = iter(jax.random.split(key, 48))
    f32 = jnp.float32

    def nrm(shape, s):
        return jax.random.normal(next(ks), shape, f32) * s

    n_pages = PAST_LEN // PAGE_SIZE
    n_phys = (DEC_BATCH * n_pages * 5) // 4
    x_prompt = nrm((BATCH, SEQ, D_MODEL), 1.0)
    x_sample = nrm((DEC_BATCH, DEC_SEQ, D_MODEL), 1.0)
    state_ssd_conv = nrm((N_SSD_LAYERS, DEC_BATCH, SSD_D_CONV - 1, SSD_CONV_DIM), 1.0)
    state_ssd_ssm = nrm((N_SSD_LAYERS, DEC_BATCH, SSD_N_HEADS, SSD_HEAD_DIM, SSD_D_STATE), 0.5)
    state_pool = nrm((N_POOL_LAYERS, DEC_BATCH, POOL_HIST, D_INNER), 1.0)
    cache_fox_k = nrm((N_FOX_LAYERS, n_phys, PAGE_SIZE, FOX_N_HEADS, FOX_HEAD_DIM), 1.0)
    cache_fox_v = nrm((N_FOX_LAYERS, n_phys, PAGE_SIZE, FOX_N_HEADS, FOX_HEAD_DIM), 1.0)
    cache_fox_logf = jax.nn.log_sigmoid(FOX_FORGET_BIAS + nrm((N_FOX_LAYERS, n_phys, PAGE_SIZE, FOX_N_HEADS), FOX_FORGET_NOISE))
    page_table = jax.random.permutation(next(ks), n_phys)[:DEC_BATCH * n_pages].reshape(DEC_BATCH, n_pages).astype(jnp.int32)

    norm_w = 1.0 + nrm((DEPTH, D_MODEL), 0.02)
    ssd_w_in = nrm((N_SSD_LAYERS, D_MODEL, SSD_IN_DIM), D_MODEL ** -0.5)
    ssd_conv_w = nrm((N_SSD_LAYERS, SSD_D_CONV, SSD_CONV_DIM), SSD_D_CONV ** -0.5)
    ssd_conv_b = nrm((N_SSD_LAYERS, SSD_CONV_DIM), 0.02)
    dt0 = jnp.exp(jax.random.uniform(next(ks), (N_SSD_LAYERS, SSD_N_HEADS), f32, np.log(1e-3), np.log(1e-1)))
    ssd_dt_bias = dt0 + jnp.log(-jnp.expm1(-dt0))
    ssd_A_log = jnp.log(jax.random.uniform(next(ks), (N_SSD_LAYERS, SSD_N_HEADS), f32, 1.0, 16.0))
    ssd_D = 1.0 + nrm((N_SSD_LAYERS, SSD_N_HEADS), 0.1)
    ssd_norm_w = 1.0 + nrm((N_SSD_LAYERS, D_INNER), 0.02)
    ssd_w_out = nrm((N_SSD_LAYERS, D_INNER, D_MODEL), D_INNER ** -0.5)
    pool_w_in = nrm((N_POOL_LAYERS, D_MODEL, 2 * D_INNER), D_MODEL ** -0.5)
    pool_w_grp = nrm((N_POOL_LAYERS, POOL_N_GROUPS, POOL_GROUP_DIM, POOL_GROUP_DIM), POOL_GROUP_DIM ** -0.5)
    pool_scale = 1.0 + nrm((N_POOL_LAYERS, D_INNER), 0.02)
    pool_w_out = nrm((N_POOL_LAYERS, D_INNER, D_MODEL), D_INNER ** -0.5)
    fox_w_in = nrm((N_FOX_LAYERS, D_MODEL, FOX_IN_DIM), D_MODEL ** -0.5)
    fox_b_f = FOX_FORGET_BIAS + nrm((N_FOX_LAYERS, FOX_N_HEADS), FOX_FORGET_NOISE)
    fox_q_norm = 1.0 + nrm((N_FOX_LAYERS, FOX_HEAD_DIM), 0.02)
    fox_k_norm = 1.0 + nrm((N_FOX_LAYERS, FOX_HEAD_DIM), 0.02)
    fox_w_out = nrm((N_FOX_LAYERS, FOX_WIDTH, D_MODEL), FOX_WIDTH ** -0.5)
    sgu_w_in = nrm((N_SGU_LAYERS, D_MODEL, 3 * D_INNER), D_MODEL ** -0.5)
    sgu_v_norm = 1.0 + nrm((N_SGU_LAYERS, D_INNER), 0.02)
    sgu_w_s = nrm((N_SGU_LAYERS, SGU_N_GROUPS, SGU_CHUNK, SGU_CHUNK), SGU_CHUNK ** -0.5)
    sgu_b_s = 1.0 + nrm((N_SGU_LAYERS, SGU_N_GROUPS, SGU_CHUNK), 0.02)
    sgu_w_out = nrm((N_SGU_LAYERS, D_INNER, D_MODEL), D_INNER ** -0.5)
    return {
        'x_prompt': x_prompt, 'x_sample': x_sample,
        'state_ssd_conv': state_ssd_conv, 'state_ssd_ssm': state_ssd_ssm, 'state_pool': state_pool,
        'cache_fox_k': cache_fox_k, 'cache_fox_v': cache_fox_v, 'cache_fox_logf': cache_fox_logf,
        'page_table': page_table,
        'norm_w': norm_w,
        'ssd_w_in': ssd_w_in, 'ssd_conv_w': ssd_conv_w, 'ssd_conv_b': ssd_conv_b,
        'ssd_dt_bias': ssd_dt_bias, 'ssd_A_log': ssd_A_log, 'ssd_D': ssd_D,
        'ssd_norm_w': ssd_norm_w, 'ssd_w_out': ssd_w_out,
        'pool_w_in': pool_w_in, 'pool_w_grp': pool_w_grp, 'pool_scale': pool_scale, 'pool_w_out': pool_w_out,
        'fox_w_in': fox_w_in, 'fox_b_f': fox_b_f, 'fox_q_norm': fox_q_norm, 'fox_k_norm': fox_k_norm,
        'fox_w_out': fox_w_out,
        'sgu_w_in': sgu_w_in, 'sgu_v_norm': sgu_v_norm, 'sgu_w_s': sgu_w_s, 'sgu_b_s': sgu_b_s,
        'sgu_w_out': sgu_w_out,
    }


def reference(x_prompt, x_sample, state_ssd_conv, state_ssd_ssm, state_pool,
              cache_fox_k, cache_fox_v, cache_fox_logf, page_table,
              norm_w, ssd_w_in, ssd_conv_w, ssd_conv_b, ssd_dt_bias, ssd_A_log, ssd_D,
              ssd_norm_w, ssd_w_out, pool_w_in, pool_w_grp, pool_scale, pool_w_out,
              fox_w_in, fox_b_f, fox_q_norm, fox_k_norm, fox_w_out,
              sgu_w_in, sgu_v_norm, sgu_w_s, sgu_b_s, sgu_w_out):

    def run(x, pos0, conv_buf, ssm_state, pool_hist, attend):
        conv_o, ssm_o, pool_o, k_o, v_o, lf_o, sgu_o = [], [], [], [], [], [], []
        for i in range(DEPTH):
            j = i // N_MIXERS
            kind = i % N_MIXERS
            h = rms_norm(x, norm_w[i])
            if kind == 0:
                o, cb, st = ssd_mixer(h, conv_buf[j], ssm_state[j], ssd_w_in[j], ssd_conv_w[j], ssd_conv_b[j],
                                      ssd_dt_bias[j], ssd_A_log[j], ssd_D[j], ssd_norm_w[j], ssd_w_out[j])
                conv_o.append(cb)
                ssm_o.append(st)
            elif kind == 1:
                o, ph = pool_mixer(h, pool_hist[j], pos0, pool_w_in[j], pool_w_grp[j], pool_scale[j], pool_w_out[j])
                pool_o.append(ph)
            elif kind == 2:
                q, k, v, z, logf = fox_project(h, fox_w_in[j], fox_b_f[j], fox_q_norm[j], fox_k_norm[j])
                o = fox_output(attend(j, q, k, v, logf), z, fox_w_out[j])
                k_o.append(k)
                v_o.append(v)
                lf_o.append(logf)
            else:
                o, vr = sgu_mixer(h, sgu_w_in[j], sgu_v_norm[j], sgu_w_s[j], sgu_b_s[j], sgu_w_out[j])
                sgu_o.append(vr)
            x = x + o
        return (x, jnp.stack(conv_o), jnp.stack(ssm_o), jnp.stack(pool_o),
                jnp.stack(k_o), jnp.stack(v_o), jnp.stack(lf_o), jnp.stack(sgu_o))

    bp = x_prompt.shape[0]
    zero_conv = jnp.zeros((N_SSD_LAYERS, bp, SSD_D_CONV - 1, SSD_CONV_DIM), x_prompt.dtype)
    zero_ssm = jnp.zeros((N_SSD_LAYERS, bp, SSD_N_HEADS, SSD_HEAD_DIM, SSD_D_STATE), jnp.float32)
    zero_pool = jnp.zeros((N_POOL_LAYERS, bp, POOL_HIST, D_INNER), x_prompt.dtype)

    def attend_prompt(j, q, k, v, lf):
        return fox_attend_prompt(q, k, v, lf)

    def attend_sample(j, q, k, v, lf):
        return fox_attend_cached(q, k, v, lf, cache_fox_k, cache_fox_v, cache_fox_logf, page_table, j)

    y_prompt, p_conv, p_ssm, p_pool, p_k, p_v, p_lf, p_sgu = run(
        x_prompt, 0, zero_conv, zero_ssm, zero_pool, attend_prompt)
    y_sample, s_conv, s_ssm, s_pool, s_k, s_v, s_lf, s_sgu = run(
        x_sample, PAST_LEN, state_ssd_conv, state_ssd_ssm, state_pool, attend_sample)
    return (y_prompt, y_sample, p_conv, p_ssm, p_pool, p_k, p_v, p_lf, p_sgu,
            s_conv, s_ssm, s_pool, s_k, s_v, s_lf, s_sgu)
```

```python
import functools

import jax
import jax.numpy as jnp
from jax import lax
from jax.experimental import pallas as pl
from jax.experimental.pallas import tpu as pltpu

F32 = jnp.float32
BF16 = jnp.bfloat16
EPS = 1e-6

D_MODEL = 2048
D_INNER = 4096
SSD_HEAD_DIM = 64
SSD_N_HEADS = 64
SSD_N_GROUPS = 8
SSD_D_STATE = 128
SSD_D_CONV = 4
SSD_CONV_DIM = D_INNER + 2 * SSD_N_GROUPS * SSD_D_STATE
POOL_WINDOWS = (2, 4, 8, 16)
POOL_GROUP_DIM = D_INNER // len(POOL_WINDOWS)
POOL_HIST = max(POOL_WINDOWS) - 1
FOX_HEAD_DIM = 128
FOX_N_HEADS = 16
FOX_WIDTH = FOX_N_HEADS * FOX_HEAD_DIM
SGU_N_GROUPS = 8
SGU_GROUP_DIM = D_INNER // SGU_N_GROUPS
PAGE_SIZE = 128
CHUNK = 128
LANES = 128
ROW_PAD = 16
PAGES_PER_STEP = 4


def _silu(x):
    return x * (1.0 / (1.0 + jnp.exp(-x)))


def _softplus(x):
    return jnp.maximum(x, 0.0) + jnp.log1p(jnp.exp(-jnp.abs(x)))


def _log_sigmoid(x):
    return jnp.minimum(x, 0.0) - jnp.log1p(jnp.exp(-jnp.abs(x)))


def _split3(x):
    hi = x.astype(BF16)
    r = x - hi.astype(F32)
    mid = r.astype(BF16)
    lo = (r - mid.astype(F32)).astype(BF16)
    return hi, mid, lo


def _dot3_left(m, x):
    hi, mid, lo = _split3(x)
    d = lambda a: jnp.dot(m, a, preferred_element_type=F32)
    return d(lo) + d(mid) + d(hi)


def _dot3_right(x, m):
    hi, mid, lo = _split3(x)
    d = lambda a: jnp.dot(a, m, preferred_element_type=F32)
    return d(lo) + d(mid) + d(hi)


def _dot_nt(a, b):
    return lax.dot_general(a, b, (((1,), (1,)), ((), ())), preferred_element_type=F32)


def _dot_tn(a, b):
    return lax.dot_general(a, b, (((0,), (0,)), ((), ())), preferred_element_type=F32)


def _norm_mm_kernel(x_ref, nw_ref, w_ref, hn_ref, o_ref, h_ref, *, hn_blocks, tn):
    j = pl.program_id(1)

    @pl.when(j == 0)
    def _():
        x = x_ref[...]
        ms = jnp.mean(x * x, axis=-1, keepdims=True)
        h_ref[...] = ((x * lax.rsqrt(ms + EPS)) * nw_ref[...]).astype(BF16)

    acc = jnp.dot(h_ref[...], w_ref[...], preferred_element_type=F32)
    if hn_blocks == 0:
        o_ref[...] = acc
    else:
        @pl.when(j >= hn_blocks)
        def _():
            o_ref[...] = acc

        @pl.when(j < hn_blocks)
        def _():
            for s in range(tn // LANES):
                cs = slice(s * LANES, (s + 1) * LANES)
                seg = acc[:, cs]
                ms = jnp.mean(seg * seg, axis=-1, keepdims=True)
                o_ref[:, cs] = (seg * lax.rsqrt(ms + EPS)) * hn_ref[:, cs]


def _norm_mm(x, nw, w, hn=None, hn_cols=0):
    m, d = x.shape
    n = w.shape[1]
    tm = min(m, 1024)
    tn = min(n, 1024)
    assert m % tm == 0 and n % tn == 0 and hn_cols % tn == 0
    if hn is None:
        hn = jnp.ones((1, n), F32)
    return pl.pallas_call(
        functools.partial(_norm_mm_kernel, hn_blocks=hn_cols // tn, tn=tn),
        grid=(m // tm, n // tn),
        in_specs=[
            pl.BlockSpec((tm, d), lambda i, j: (i, 0)),
            pl.BlockSpec((1, d), lambda i, j: (0, 0)),
            pl.BlockSpec((d, tn), lambda i, j: (0, j)),
            pl.BlockSpec((1, tn), lambda i, j: (0, j)),
        ],
        out_specs=pl.BlockSpec((tm, tn), lambda i, j: (i, j)),
        out_shape=jax.ShapeDtypeStruct((m, n), F32),
        scratch_shapes=[pltpu.VMEM((tm, d), BF16)],
        compiler_params=pltpu.CompilerParams(dimension_semantics=("arbitrary", "arbitrary")),
        name="norm_mm",
    )(x, nw.reshape(1, d), w, hn)


def _mm_res_kernel(l_ref, w_ref, r_ref, o_ref):
    o_ref[...] = r_ref[...] + jnp.dot(l_ref[...], w_ref[...], preferred_element_type=F32)


def _mm_res(lhs, w, res):
    m, k = lhs.shape
    n = w.shape[1]
    tm = min(m, 512)
    tn = min(n, 1024)
    assert m % tm == 0 and n % tn == 0
    return pl.pallas_call(
        _mm_res_kernel,
        grid=(m // tm, n // tn),
        in_specs=[
            pl.BlockSpec((tm, k), lambda i, j: (i, 0)),
            pl.BlockSpec((k, tn), lambda i, j: (0, j)),
            pl.BlockSpec((tm, tn), lambda i, j: (i, j)),
        ],
        out_specs=pl.BlockSpec((tm, tn), lambda i, j: (i, j)),
        out_shape=jax.ShapeDtypeStruct((m, n), F32),
        compiler_params=pltpu.CompilerParams(dimension_semantics=("arbitrary", "arbitrary")),
        name="mm_res",
    )(lhs, w, res)


def _ssd_kernel(z_ref, xbc_ref, dt_ref, cw_ref, cb_ref, dtb_ref, alog_ref, dexp_ref, nw_ref,
                cinit_ref, sinit_ref, y_ref, hout_ref,
                ext_ref, act_ref, yacc_ref, ht_ref, *, s_true, n_chunks):
    c = pl.program_id(1)
    L = CHUNK
    n_blk = D_INNER // LANES

    @pl.when(c == 0)
    def _():
        ext_ref[0:8, :] = cinit_ref[0]
        for i in range(n_blk):
            cs = slice(i * LANES, (i + 1) * LANES)
            ht_ref[:, cs] = sinit_ref[0, cs, :].T

    ext_ref[8:8 + L, :] = xbc_ref[0]
    slab = 512
    for s in range(SSD_CONV_DIM // slab):
        cs = slice(s * slab, (s + 1) * slab)
        acc = ext_ref[5:5 + L, cs] * cw_ref[0:1, cs]
        for k in range(1, SSD_D_CONV):
            acc = acc + ext_ref[5 + k:5 + k + L, cs] * cw_ref[k:k + 1, cs]
        act_ref[:, cs] = _silu(acc + cb_ref[:, cs])
    ext_ref[0:8, :] = ext_ref[L:L + 8, :]

    row = lax.broadcasted_iota(jnp.int32, (L, LANES), 0)
    col = lax.broadcasted_iota(jnp.int32, (L, LANES), 1)
    tril = row >= col
    lo = col < SSD_HEAD_DIM
    dtv = _softplus(dt_ref[0] + dtb_ref[...])
    dtv = jnp.where(row + c * L < s_true, dtv, 0.0)
    a = dtv * (-jnp.exp(alog_ref[...]))
    tri = jnp.where(tril, 1.0, 0.0).astype(BF16)
    cum = _dot3_left(tri, a)
    cum_t = cum.T
    dt_t = dtv.T
    ecum = jnp.exp(cum)
    clast = cum[L - 1:L, :]
    tailw = jnp.exp(clast - cum) * dtv
    elast = jnp.exp(clast)

    x_off = 0
    b_off = D_INNER
    c_off = D_INNER + SSD_N_GROUPS * SSD_D_STATE
    heads_per_group = SSD_N_HEADS // SSD_N_GROUPS
    gw = heads_per_group * SSD_HEAD_DIM
    for g in range(SSD_N_GROUPS):
        bb = act_ref[:, b_off + g * SSD_D_STATE:b_off + (g + 1) * SSD_D_STATE].astype(BF16)
        cc = act_ref[:, c_off + g * SSD_D_STATE:c_off + (g + 1) * SSD_D_STATE].astype(BF16)
        cb = _dot_nt(cc, bb)
        gs = slice(g * gw, (g + 1) * gw)
        h_old = ht_ref[:, gs]
        y_state = jnp.dot(cc, h_old.astype(BF16), preferred_element_type=F32)
        xw_parts = []
        el_parts = []
        for i in range(heads_per_group // 2):
            h0 = g * heads_per_group + 2 * i
            h1 = h0 + 1
            pc = slice(x_off + h0 * SSD_HEAD_DIM, x_off + h0 * SSD_HEAD_DIM + LANES)
            xp = act_ref[:, pc]

            def head_mix(h):
                seg = cum[:, h:h + 1] - cum_t[h:h + 1, :]
                return jnp.exp(jnp.where(tril, seg, -jnp.inf)) * cb * dt_t[h:h + 1, :]

            m_cat = jnp.concatenate([head_mix(h0), head_mix(h1)], axis=1).astype(BF16)
            x_bd = jnp.concatenate([jnp.where(lo, xp, 0.0), jnp.where(lo, 0.0, xp)], axis=0).astype(BF16)
            yp = jnp.dot(m_cat, x_bd, preferred_element_type=F32)
            ep = jnp.where(lo, ecum[:, h0:h0 + 1], ecum[:, h1:h1 + 1])
            yp = yp + y_state[:, i * LANES:(i + 1) * LANES] * ep
            yp = yp + dexp_ref[:, pc] * xp
            yacc_ref[:, pc] = yp
            tw = jnp.where(lo, tailw[:, h0:h0 + 1], tailw[:, h1:h1 + 1])
            xw_parts.append((xp * tw).astype(BF16))
            el_parts.append(jnp.where(lo[0:1, :], elast[:, h0:h0 + 1], elast[:, h1:h1 + 1]))
        xw = jnp.concatenate(xw_parts, axis=1)
        el = jnp.concatenate(el_parts, axis=1)
        ht_ref[:, gs] = h_old * el + _dot_tn(bb, xw)

    gated = yacc_ref[...] * _silu(z_ref[0])
    ms = jnp.mean(gated * gated, axis=-1, keepdims=True)
    y_ref[0] = ((gated * lax.rsqrt(ms + EPS)) * nw_ref[...]).astype(BF16)

    @pl.when(c == n_chunks - 1)
    def _():
        for i in range(n_blk):
            cs = slice(i * LANES, (i + 1) * LANES)
            hout_ref[0, cs, :] = ht_ref[:, cs].T


def _ssd_core(z, xbc, dtr, conv_w, conv_b, dt_bias, a_log, d_skip, norm_w, conv_init, ssm_init, s_true):
    b, s, _ = z.shape
    nc = s // CHUNK
    hp = SSD_N_HEADS * SSD_HEAD_DIM
    pad_h = LANES - SSD_N_HEADS
    const = lambda *shape: pl.BlockSpec(shape, lambda bi, ci: (0,) * len(shape))
    return pl.pallas_call(
        functools.partial(_ssd_kernel, s_true=s_true, n_chunks=nc),
        grid=(b, nc),
        in_specs=[
            pl.BlockSpec((1, CHUNK, D_INNER), lambda bi, ci: (bi, ci, 0)),
            pl.BlockSpec((1, CHUNK, SSD_CONV_DIM), lambda bi, ci: (bi, ci, 0)),
            pl.BlockSpec((1, CHUNK, LANES), lambda bi, ci: (bi, ci, 0)),
            const(SSD_D_CONV, SSD_CONV_DIM),
            const(1, SSD_CONV_DIM),
            const(1, LANES),
            const(1, LANES),
            const(1, D_INNER),
            const(1, D_INNER),
            pl.BlockSpec((1, 8, SSD_CONV_DIM), lambda bi, ci: (bi, 0, 0)),
            pl.BlockSpec((1, hp, SSD_D_STATE), lambda bi, ci: (bi, 0, 0)),
        ],
        out_specs=[
            pl.BlockSpec((1, CHUNK, D_INNER), lambda bi, ci: (bi, ci, 0)),
            pl.BlockSpec((1, hp, SSD_D_STATE), lambda bi, ci: (bi, 0, 0)),
        ],
        out_shape=[
            jax.ShapeDtypeStruct((b, s, D_INNER), BF16),
            jax.ShapeDtypeStruct((b, hp, SSD_D_STATE), F32),
        ],
        scratch_shapes=[
            pltpu.VMEM((CHUNK + 8, SSD_CONV_DIM), F32),
            pltpu.VMEM((CHUNK, SSD_CONV_DIM), F32),
            pltpu.VMEM((CHUNK, D_INNER), F32),
            pltpu.VMEM((SSD_D_STATE, hp), F32),
        ],
        compiler_params=pltpu.CompilerParams(dimension_semantics=("arbitrary", "arbitrary")),
        name="ssd_core",
    )(z, xbc, dtr, conv_w, conv_b.reshape(1, -1),
      jnp.pad(dt_bias, (0, pad_h)).reshape(1, LANES), jnp.pad(a_log, (0, pad_h)).reshape(1, LANES),
      jnp.repeat(d_skip, SSD_HEAD_DIM).reshape(1, hp), norm_w.reshape(1, -1), conv_init, ssm_init)


def _pool_kernel(p_ref, z_ref, hinit_ref, wg_ref, sc_ref, o_ref, ext_ref, *, tm, pos0):
    t = pl.program_id(1)
    hist_rows = POOL_HIST + 1

    @pl.when(t == 0)
    def _():
        ext_ref[0:hist_rows, :] = hinit_ref[0]

    ext_ref[hist_rows:hist_rows + tm, :] = p_ref[0]
    pos = lax.broadcasted_iota(jnp.int32, (tm, POOL_GROUP_DIM), 0) + t * tm + pos0
    for g, w in enumerate(POOL_WINDOWS):
        cs = slice(g * POOL_GROUP_DIM, (g + 1) * POOL_GROUP_DIM)
        win = ext_ref[hist_rows:hist_rows + tm, cs]
        for j in range(1, w):
            win = win + ext_ref[hist_rows - j:hist_rows - j + tm, cs]
        count = jnp.minimum(pos + 1, w).astype(F32)
        diff = (win / count - p_ref[0, :, cs]).astype(BF16)
        mixed = jnp.dot(diff, wg_ref[g], preferred_element_type=F32) * sc_ref[:, cs]
        o_ref[0, :, cs] = (mixed * _silu(z_ref[0, :, cs])).astype(BF16)
    ext_ref[0:hist_rows, :] = ext_ref[tm:tm + hist_rows, :]


def _pool_core(pz, hist_init, w_grp, scale, pos0):
    b, s, _ = pz.shape
    tm = min(s, 256)
    return pl.pallas_call(
        functools.partial(_pool_kernel, tm=tm, pos0=pos0),
        grid=(b, s // tm),
        in_specs=[
            pl.BlockSpec((1, tm, D_INNER), lambda bi, ti: (bi, ti, 0)),
            pl.BlockSpec((1, tm, D_INNER), lambda bi, ti: (bi, ti, 1)),
            pl.BlockSpec((1, POOL_HIST + 1, D_INNER), lambda bi, ti: (bi, 0, 0)),
            pl.BlockSpec((len(POOL_WINDOWS), POOL_GROUP_DIM, POOL_GROUP_DIM), lambda bi, ti: (0, 0, 0)),
            pl.BlockSpec((1, D_INNER), lambda bi, ti: (0, 0)),
        ],
        out_specs=pl.BlockSpec((1, tm, D_INNER), lambda bi, ti: (bi, ti, 0)),
        out_shape=jax.ShapeDtypeStruct((b, s, D_INNER), BF16),
        scratch_shapes=[pltpu.VMEM((tm + POOL_HIST + 1, D_INNER), F32)],
        compiler_params=pltpu.CompilerParams(dimension_semantics=("arbitrary", "arbitrary")),
        name="pool_core",
    )(pz, pz, hist_init, w_grp, scale.reshape(1, -1))


def _sgu_kernel(u_ref, v_ref, z_ref, vnw_ref, ws_ref, bst_ref, g_ref, vout_ref, *, n_chunks):
    c = pl.program_id(1)
    v = v_ref[0]
    ms = jnp.mean(v * v, axis=-1, keepdims=True)
    vn = (v * lax.rsqrt(ms + EPS)) * vnw_ref[...]
    row = lax.broadcasted_iota(jnp.int32, (CHUNK, CHUNK), 0)
    col = lax.broadcasted_iota(jnp.int32, (CHUNK, CHUNK), 1)
    tril = row >= col
    for g in range(SGU_N_GROUPS):
        cs = slice(g * SGU_GROUP_DIM, (g + 1) * SGU_GROUP_DIM)
        w = jnp.where(tril, ws_ref[g], 0.0).astype(BF16)
        mixed = jnp.dot(w, vn[:, cs].astype(BF16), preferred_element_type=F32) + bst_ref[:, g:g + 1]
        g_ref[0, :, cs] = ((u_ref[0, :, cs] * mixed) * _silu(z_ref[0, :, cs])).astype(BF16)

    @pl.when(c == n_chunks - 1)
    def _():
        vout_ref[0] = vn


def _sgu_core(uvz, v_norm, w_s, b_s):
    b, s, _ = uvz.shape
    nc = s // CHUNK
    blk = lambda k: pl.BlockSpec((1, CHUNK, D_INNER), lambda bi, ci, k=k: (bi, ci, k))
    return pl.pallas_call(
        functools.partial(_sgu_kernel, n_chunks=nc),
        grid=(b, nc),
        in_specs=[
            blk(0), blk(1), blk(2),
            pl.BlockSpec((1, D_INNER), lambda bi, ci: (0, 0)),
            pl.BlockSpec((SGU_N_GROUPS, CHUNK, CHUNK), lambda bi, ci: (0, 0, 0)),
            pl.BlockSpec((CHUNK, SGU_N_GROUPS), lambda bi, ci: (0, 0)),
        ],
        out_specs=[
            pl.BlockSpec((1, CHUNK, D_INNER), lambda bi, ci: (bi, ci, 0)),
            pl.BlockSpec((1, CHUNK, D_INNER), lambda bi, ci: (bi, 0, 0)),
        ],
        out_shape=[
            jax.ShapeDtypeStruct((b, s, D_INNER), BF16),
            jax.ShapeDtypeStruct((b, CHUNK, D_INNER), F32),
        ],
        compiler_params=pltpu.CompilerParams(dimension_semantics=("arbitrary", "arbitrary")),
        name="sgu_core",
    )(uvz, uvz, uvz, v_norm.reshape(1, -1), w_s, jnp.transpose(b_s))


def _fox_gate_kernel(fl_ref, bf_ref, lf_ref, f_ref, ft_ref, carry_ref, *, tt):
    t = pl.program_id(1)

    @pl.when(t == 0)
    def _():
        carry_ref[...] = jnp.zeros_like(carry_ref)

    lf = _log_sigmoid(fl_ref[0] + bf_ref[...])
    row = lax.broadcasted_iota(jnp.int32, (tt, tt), 0)
    col = lax.broadcasted_iota(jnp.int32, (tt, tt), 1)
    tri = jnp.where(row >= col, 1.0, 0.0).astype(BF16)
    cum = _dot3_left(tri, lf) + carry_ref[0:1, :]
    carry_ref[0:1, :] = cum[tt - 1:tt, :]
    lf_ref[0] = lf
    f_ref[0] = cum
    ft_ref[0] = cum.T


def _fox_gates(f_logit, b_f):
    b, s, _ = f_logit.shape
    tt = min(s, 256)
    return pl.pallas_call(
        functools.partial(_fox_gate_kernel, tt=tt),
        grid=(b, s // tt),
        in_specs=[
            pl.BlockSpec((1, tt, LANES), lambda bi, ti: (bi, ti, 0)),
            pl.BlockSpec((1, LANES), lambda bi, ti: (0, 0)),
        ],
        out_specs=[
            pl.BlockSpec((1, tt, LANES), lambda bi, ti: (bi, ti, 0)),
            pl.BlockSpec((1, tt, LANES), lambda bi, ti: (bi, ti, 0)),
            pl.BlockSpec((1, LANES, tt), lambda bi, ti: (bi, 0, ti)),
        ],
        out_shape=[
            jax.ShapeDtypeStruct((b, s, LANES), F32),
            jax.ShapeDtypeStruct((b, s, LANES), F32),
            jax.ShapeDtypeStruct((b, LANES, s), F32),
        ],
        scratch_shapes=[pltpu.VMEM((8, LANES), F32)],
        compiler_params=pltpu.CompilerParams(dimension_semantics=("arbitrary", "arbitrary")),
        name="fox_gates",
    )(f_logit, b_f)


def _fox_attn_kernel(q_ref, k_ref, v_ref, z_ref, f_ref, ft_ref, o_ref, m_ref, l_ref, acc_ref, *, tq, tk):
    qi = pl.program_id(1)
    ki = pl.program_id(2)
    scale = FOX_HEAD_DIM ** -0.5

    @pl.when(ki == 0)
    def _():
        m_ref[...] = jnp.full_like(m_ref, -jnp.inf)
        l_ref[...] = jnp.zeros_like(l_ref)
        acc_ref[...] = jnp.zeros_like(acc_ref)

    @pl.when(ki <= qi)
    def _():
        qpos = qi * tq + lax.broadcasted_iota(jnp.int32, (tq, tk), 0)
        kpos = ki * tk + lax.broadcasted_iota(jnp.int32, (tq, tk), 1)
        mask = kpos <= qpos
        for h in range(FOX_N_HEADS):
            hs = slice(h * FOX_HEAD_DIM, (h + 1) * FOX_HEAD_DIM)
            q = q_ref[0, :, hs].astype(BF16)
            k = k_ref[0, :, hs].astype(BF16)
            s = _dot_nt(q, k) * scale
            s = s + f_ref[0, :, h:h + 1] - ft_ref[0, h:h + 1, :]
            s = jnp.where(mask, s, -jnp.inf)
            m_prev = m_ref[h][:, 0:1]
            l_prev = l_ref[h][:, 0:1]
            m_new = jnp.maximum(m_prev, jnp.max(s, axis=-1, keepdims=True))
            alpha = jnp.exp(m_prev - m_new)
            p = jnp.exp(s - m_new)
            l_new = alpha * l_prev + jnp.sum(p, axis=-1, keepdims=True)
            pv = jnp.dot(p.astype(BF16), v_ref[0, :, hs].astype(BF16), preferred_element_type=F32)
            acc_ref[:, hs] = alpha * acc_ref[:, hs] + pv
            m_ref[h] = jnp.broadcast_to(m_new, (tq, LANES))
            l_ref[h] = jnp.broadcast_to(l_new, (tq, LANES))

    @pl.when(ki == qi)
    def _():
        for h in range(FOX_N_HEADS):
            hs = slice(h * FOX_HEAD_DIM, (h + 1) * FOX_HEAD_DIM)
            a = acc_ref[:, hs] / l_ref[h][:, 0:1]
            o_ref[0, :, hs] = (a * _silu(z_ref[0, :, hs])).astype(BF16)


def _fox_attn(qkvz, f, ft):
    b, s, _ = qkvz.shape
    tq = tk = min(s, 512)
    nq = s // tq
    kv = lambda k: pl.BlockSpec((1, tk, FOX_WIDTH), lambda bi, qi, ki, k=k: (bi, jnp.minimum(ki, qi), k))
    qz = lambda k: pl.BlockSpec((1, tq, FOX_WIDTH), lambda bi, qi, ki, k=k: (bi, qi, k))
    return pl.pallas_call(
        functools.partial(_fox_attn_kernel, tq=tq, tk=tk),
        grid=(b, nq, nq),
        in_specs=[
            qz(0), kv(1), kv(2), qz(3),
            pl.BlockSpec((1, tq, LANES), lambda bi, qi, ki: (bi, qi, 0)),
            pl.BlockSpec((1, LANES, tk), lambda bi, qi, ki: (bi, 0, jnp.minimum(ki, qi))),
        ],
        out_specs=pl.BlockSpec((1, tq, FOX_WIDTH), lambda bi, qi, ki: (bi, qi, 0)),
        out_shape=jax.ShapeDtypeStruct((b, s, FOX_WIDTH), BF16),
        scratch_shapes=[
            pltpu.VMEM((FOX_N_HEADS, tq, LANES), F32),
            pltpu.VMEM((FOX_N_HEADS, tq, LANES), F32),
            pltpu.VMEM((tq, FOX_WIDTH), F32),
        ],
        compiler_params=pltpu.CompilerParams(dimension_semantics=("arbitrary", "arbitrary", "arbitrary")),
        name="fox_attn",
    )(qkvz, qkvz, qkvz, qkvz, f, ft)


def _fox_decode_kernel(pt_ref, *refs, n_pages):
    pps = PAGES_PER_STEP
    k_refs = refs[0:pps]
    v_refs = refs[pps:2 * pps]
    lf_refs = refs[2 * pps:3 * pps]
    (q_ref, kn_ref, vn_ref, z_ref, fl_ref, bf_ref, o_ref, lfn_ref,
     m_ref, l_ref, acc_ref, carry_ref) = refs[3 * pps:]
    del pt_ref
    j = pl.program_id(1)
    n_steps = n_pages // pps
    nh = FOX_N_HEADS
    scale = FOX_HEAD_DIM ** -0.5
    hrow = lax.broadcasted_iota(jnp.int32, (nh, LANES), 0)
    lane = lax.broadcasted_iota(jnp.int32, (nh, LANES), 1)
    own = (lane % nh) == hrow

    @pl.when(j == 0)
    def _():
        lfn = _log_sigmoid(fl_ref[0] + bf_ref[...])
        lfn_ref[0] = lfn
        fn_col = jnp.sum(jnp.where(lane == hrow, lfn, 0.0), axis=-1, keepdims=True)
        s_new = jnp.sum(q_ref[0] * kn_ref[0], axis=-1, keepdims=True) * scale - fn_col
        m_ref[...] = jnp.broadcast_to(s_new, (nh, LANES))
        l_ref[...] = jnp.ones_like(l_ref)
        acc_ref[...] = vn_ref[0]
        carry_ref[...] = jnp.zeros_like(carry_ref)

    rows_pp = PAGE_SIZE * nh // LANES
    n_rows = pps * rows_pp
    x4 = jnp.concatenate([lf_refs[i][0] for i in range(pps)], axis=0)
    src = lax.broadcasted_iota(jnp.int32, (LANES, 2 * LANES), 0)
    dst = lax.broadcasted_iota(jnp.int32, (LANES, 2 * LANES), 1)
    same_head = (src % nh) == (dst % nh)
    w12 = jnp.where(same_head & ((dst >= LANES) | (src > dst)), 1.0, 0.0).astype(BF16)
    r12 = _dot3_right(x4, w12)
    within = r12[:, :LANES]
    tot = r12[:, LANES:]
    rr = lax.broadcasted_iota(jnp.int32, (n_rows, LANES), 0)
    cc = lax.broadcasted_iota(jnp.int32, (n_rows, LANES), 1)
    later = (cc < n_rows) & ((cc // rows_pp < rr // rows_pp) | ((cc // rows_pp == rr // rows_pp) & (cc > rr)))
    tot_pad = jnp.concatenate([tot, jnp.zeros((LANES - n_rows, LANES), F32)], axis=0)
    carry = carry_ref[0:1, :]
    tail4 = within + _dot3_left(jnp.where(later, 1.0, 0.0).astype(BF16), tot_pad) + carry
    carry_ref[0:1, :] = carry + jnp.sum(tot, axis=0, keepdims=True)

    qb = q_ref[0].astype(BF16)
    scores = []
    for i in range(pps):
        s = _dot_nt(qb, k_refs[i][0].astype(BF16)) * scale
        parts = []
        for a in range(rows_pp):
            r = i * rows_pp + a
            parts.append(jnp.where(own, s[:, a * LANES:(a + 1) * LANES] + tail4[r:r + 1, :], -jnp.inf))
        scores.append(jnp.concatenate(parts, axis=1))
    m_prev = m_ref[:, 0:1]
    m_new = m_prev
    for s in scores:
        m_new = jnp.maximum(m_new, jnp.max(s, axis=-1, keepdims=True))
    alpha = jnp.exp(m_prev - m_new)
    l_new = alpha * l_ref[:, 0:1]
    acc = alpha * acc_ref[...]
    for i, s in enumerate(scores):
        p = jnp.exp(s - m_new)
        l_new = l_new + jnp.sum(p, axis=-1, keepdims=True)
        acc = acc + jnp.dot(p.astype(BF16), v_refs[i][0].astype(BF16), preferred_element_type=F32)
    m_ref[...] = jnp.broadcast_to(m_new, (nh, LANES))
    l_ref[...] = jnp.broadcast_to(l_new, (nh, LANES))
    acc_ref[...] = acc

    @pl.when(j == n_steps - 1)
    def _():
        o_ref[0] = (acc / l_new) * _silu(z_ref[0])


def _fox_decode(q, k_new, v_new, z, f_logit, b_f, cache_k, cache_v, cache_lf, page_table):
    b = q.shape[0]
    n_pages = page_table.shape[1]
    pps = PAGES_PER_STEP
    assert n_pages % pps == 0
    n_steps = n_pages // pps
    keys = PAGE_SIZE * FOX_N_HEADS

    def page_map(i):
        return lambda bi, j, pt: (pt[bi * n_pages + (n_pages - 1 - (pps * j + i))], 0, 0)

    kv_spec = lambda i: pl.BlockSpec((1, keys, FOX_HEAD_DIM), page_map(i))
    lf_spec = lambda i: pl.BlockSpec((1, keys // LANES, LANES), page_map(i))
    tok = lambda r: pl.BlockSpec((1, r, LANES), lambda bi, j, pt: (bi, 0, 0))
    grid_spec = pltpu.PrefetchScalarGridSpec(
        num_scalar_prefetch=1,
        grid=(b, n_steps),
        in_specs=([kv_spec(i) for i in range(pps)] + [kv_spec(i) for i in range(pps)]
                  + [lf_spec(i) for i in range(pps)]
                  + [tok(FOX_N_HEADS), tok(FOX_N_HEADS), tok(FOX_N_HEADS), tok(FOX_N_HEADS), tok(1),
                     pl.BlockSpec((1, LANES), lambda bi, j, pt: (0, 0))]),
        out_specs=[tok(FOX_N_HEADS), tok(1)],
        scratch_shapes=[
            pltpu.VMEM((FOX_N_HEADS, LANES), F32),
            pltpu.VMEM((FOX_N_HEADS, LANES), F32),
            pltpu.VMEM((FOX_N_HEADS, FOX_HEAD_DIM), F32),
            pltpu.VMEM((8, LANES), F32),
        ],
    )
    out, lfn = pl.pallas_call(
        functools.partial(_fox_decode_kernel, n_pages=n_pages),
        grid_spec=grid_spec,
        out_shape=[jax.ShapeDtypeStruct((b, FOX_N_HEADS, FOX_HEAD_DIM), F32),
                   jax.ShapeDtypeStruct((b, 1, LANES), F32)],
        compiler_params=pltpu.CompilerParams(dimension_semantics=("arbitrary", "arbitrary")),
        name="fox_decode",
    )(page_table.reshape(-1), *([cache_k] * pps), *([cache_v] * pps), *([cache_lf] * pps),
      q, k_new, v_new, z, f_logit.reshape(b, 1, LANES), b_f)
    return out, lfn.reshape(b, LANES)


def _pad_rows(a, rows):
    return jnp.pad(a, ((0, rows - a.shape[0]),) + ((0, 0),) * (a.ndim - 1))


def _pad_cols(a, cols):
    return jnp.pad(a, ((0, 0),) * (a.ndim - 1) + ((0, cols - a.shape[-1]),))


def _run(x, s_true, pos0, conv_buf, ssm_state, pool_hist, attend, wts):
    b = x.shape[0]
    m_true = b * s_true
    m = max(m_true, ROW_PAD)
    s_pad = max(s_true, CHUNK)
    xf = _pad_rows(x.reshape(m_true, D_MODEL), m)

    def seq(a):
        a = a[:m_true].reshape(b, s_true, a.shape[-1])
        return jnp.pad(a, ((0, 0), (0, s_pad - s_true), (0, 0)))

    def rows(a):
        return _pad_rows(a[:, :s_true].reshape(m_true, a.shape[-1]), m)

    z = _norm_mm(xf, wts['norm_w'][0], wts['ssd_w_z'])
    xbc = _norm_mm(xf, wts['norm_w'][0], wts['ssd_w_xbc'])
    dtr = _norm_mm(xf, wts['norm_w'][0], wts['ssd_w_dt'])
    xbc_s = seq(xbc)
    conv_init = jnp.pad(conv_buf, ((0, 0), (8 - (SSD_D_CONV - 1), 0), (0, 0)))
    y, h_new = _ssd_core(seq(z), xbc_s, seq(dtr), wts['ssd_conv_w'], wts['ssd_conv_b'], wts['ssd_dt_bias'],
                         wts['ssd_A_log'], wts['ssd_D'], wts['ssd_norm_w'], conv_init,
                         ssm_state.reshape(b, SSD_N_HEADS * SSD_HEAD_DIM, SSD_D_STATE), s_true)
    xf = _mm_res(rows(y), wts['ssd_w_out'], xf)
    conv_o = jnp.concatenate([conv_buf, xbc_s[:, :s_true]], axis=1)[:, -(SSD_D_CONV - 1):]
    ssm_o = h_new.reshape(b, SSD_N_HEADS, SSD_HEAD_DIM, SSD_D_STATE)

    pz = seq(_norm_mm(xf, wts['norm_w'][1], wts['pool_w_in']))
    hist_init = jnp.pad(pool_hist, ((0, 0), (1, 0), (0, 0)))
    g = _pool_core(pz, hist_init, wts['pool_w_grp'], wts['pool_scale'], pos0)
    xf = _mm_res(rows(g), wts['pool_w_out'], xf)
    pool_o = jnp.concatenate([pool_hist, pz[:, :s_true, :D_INNER]], axis=1)[:, -POOL_HIST:]

    qkvz = _norm_mm(xf, wts['norm_w'][2], wts['fox_w_qkvz'], hn=wts['fox_hn'], hn_cols=2 * FOX_WIDTH)
    f_logit = _norm_mm(xf, wts['norm_w'][2], wts['fox_w_f'])
    g, logf = attend(qkvz[:m_true], f_logit[:m_true])
    xf = _mm_res(_pad_rows(g, m), wts['fox_w_out'], xf)
    k_o = qkvz[:m_true, FOX_WIDTH:2 * FOX_WIDTH].reshape(b, s_true, FOX_N_HEADS, FOX_HEAD_DIM)
    v_o = qkvz[:m_true, 2 * FOX_WIDTH:3 * FOX_WIDTH].reshape(b, s_true, FOX_N_HEADS, FOX_HEAD_DIM)
    lf_o = logf[:, :FOX_N_HEADS].reshape(b, s_true, FOX_N_HEADS)

    uvz = seq(_norm_mm(xf, wts['norm_w'][3], wts['sgu_w_in']))
    g, v_last = _sgu_core(uvz, wts['sgu_v_norm'], wts['sgu_w_s'], wts['sgu_b_s'])
    xf = _mm_res(rows(g), wts['sgu_w_out'], xf)
    n_last = s_true - ((s_true - 1) // CHUNK) * CHUNK
    sgu_o = v_last[:, :n_last]

    y_out = xf[:m_true].reshape(b, s_true, D_MODEL)
    return (y_out, conv_o[None], ssm_o[None], pool_o[None], k_o[None], v_o[None], lf_o[None], sgu_o[None])


def kernel(x_prompt, x_sample, state_ssd_conv, state_ssd_ssm, state_pool, cache_fox_k, cache_fox_v, cache_fox_logf, page_table, norm_w, ssd_w_in, ssd_conv_w, ssd_conv_b, ssd_dt_bias, ssd_A_log, ssd_D, ssd_norm_w, ssd_w_out, pool_w_in, pool_w_grp, pool_scale, pool_w_out, fox_w_in, fox_b_f, fox_q_norm, fox_k_norm, fox_w_out, sgu_w_in, sgu_v_norm, sgu_w_s, sgu_b_s, sgu_w_out):
    assert norm_w.shape[0] == 4, "one layer of each mixer kind"
    bf = lambda a: a.astype(BF16)
    w_in = ssd_w_in[0]
    fox_in = fox_w_in[0]
    wts = {
        'norm_w': norm_w,
        'ssd_w_z': bf(w_in[:, :D_INNER]),
        'ssd_w_xbc': bf(w_in[:, D_INNER:D_INNER + SSD_CONV_DIM]),
        'ssd_w_dt': bf(_pad_cols(w_in[:, D_INNER + SSD_CONV_DIM:], LANES)),
        'ssd_conv_w': ssd_conv_w[0], 'ssd_conv_b': ssd_conv_b[0], 'ssd_dt_bias': ssd_dt_bias[0],
        'ssd_A_log': ssd_A_log[0], 'ssd_D': ssd_D[0], 'ssd_norm_w': ssd_norm_w[0],
        'ssd_w_out': bf(ssd_w_out[0]),
        'pool_w_in': bf(pool_w_in[0]), 'pool_w_grp': bf(pool_w_grp[0]), 'pool_scale': pool_scale[0],
        'pool_w_out': bf(pool_w_out[0]),
        'fox_w_qkvz': bf(fox_in[:, :4 * FOX_WIDTH]),
        'fox_w_f': bf(_pad_cols(fox_in[:, 4 * FOX_WIDTH:], LANES)),
        'fox_hn': jnp.concatenate([jnp.tile(fox_q_norm[0], FOX_N_HEADS), jnp.tile(fox_k_norm[0], FOX_N_HEADS),
                                   jnp.ones((2 * FOX_WIDTH,), F32)]).reshape(1, 4 * FOX_WIDTH),
        'fox_w_out': bf(fox_w_out[0]),
        'sgu_w_in': bf(sgu_w_in[0]), 'sgu_v_norm': sgu_v_norm[0], 'sgu_w_s': sgu_w_s[0], 'sgu_b_s': sgu_b_s[0],
        'sgu_w_out': bf(sgu_w_out[0]),
    }
    b_f = _pad_cols(fox_b_f[0].reshape(1, FOX_N_HEADS), LANES)

    bp, sp, _ = x_prompt.shape
    bs = x_sample.shape[0]
    assert x_sample.shape[1] == 1, "the cached group decodes one token per sequence"

    def attend_prompt(qkvz, f_logit):
        lf, f, ft = _fox_gates(f_logit.reshape(bp, sp, LANES), b_f)
        g = _fox_attn(qkvz.reshape(bp, sp, 4 * FOX_WIDTH), f, ft)
        return g.reshape(bp * sp, FOX_WIDTH), lf.reshape(bp * sp, LANES)

    n_phys = cache_fox_k.shape[1]
    keys = PAGE_SIZE * FOX_N_HEADS
    ck = cache_fox_k[0].reshape(n_phys, keys, FOX_HEAD_DIM)
    cv = cache_fox_v[0].reshape(n_phys, keys, FOX_HEAD_DIM)
    clf = cache_fox_logf[0].reshape(n_phys, keys // LANES, LANES)

    def attend_sample(qkvz, f_logit):
        q, k, v, z = (qkvz[:, i * FOX_WIDTH:(i + 1) * FOX_WIDTH].reshape(bs, FOX_N_HEADS, FOX_HEAD_DIM)
                      for i in range(4))
        g, lf = _fox_decode(q, k, v, z, f_logit, b_f, ck, cv, clf, page_table)
        return g.reshape(bs, FOX_WIDTH).astype(BF16), lf

    zero_conv = jnp.zeros((bp, SSD_D_CONV - 1, SSD_CONV_DIM), F32)
    zero_ssm = jnp.zeros((bp, SSD_N_HEADS, SSD_HEAD_DIM, SSD_D_STATE), F32)
    zero_pool = jnp.zeros((bp, POOL_HIST, D_INNER), F32)
    past_len = page_table.shape[1] * PAGE_SIZE
    outs_p = _run(x_prompt, sp, 0, zero_conv, zero_ssm, zero_pool, attend_prompt, wts)
    outs_s = _run(x_sample, 1, past_len, state_ssd_conv[0], state_ssd_ssm[0], state_pool[0], attend_sample, wts)
    return (outs_p[0], outs_s[0]) + outs_p[1:] + outs_s[1:]
```

```python
import functools

import jax
import jax.numpy as jnp
from jax import lax
from jax.experimental import pallas as pl
from jax.experimental.pallas import tpu as pltpu

F32 = jnp.float32
BF16 = jnp.bfloat16
EPS = 1e-6

D_MODEL = 2048
D_INNER = 4096
SSD_HEAD_DIM = 64
SSD_N_HEADS = 64
SSD_N_GROUPS = 8
SSD_D_STATE = 128
SSD_D_CONV = 4
SSD_CONV_DIM = D_INNER + 2 * SSD_N_GROUPS * SSD_D_STATE
POOL_WINDOWS = (2, 4, 8, 16)
POOL_GROUP_DIM = D_INNER // len(POOL_WINDOWS)
POOL_HIST = max(POOL_WINDOWS) - 1
FOX_HEAD_DIM = 128
FOX_N_HEADS = 16
FOX_WIDTH = FOX_N_HEADS * FOX_HEAD_DIM
SGU_N_GROUPS = 8
SGU_GROUP_DIM = D_INNER // SGU_N_GROUPS
PAGE_SIZE = 128
CHUNK = 128
LANES = 128
ROW_PAD = 16
PAGES_PER_STEP = 4
XBC_BLOCK = 2048


def _silu(x):
    return x * (1.0 / (1.0 + jnp.exp(-x)))


def _softplus(x):
    return jnp.maximum(x, 0.0) + jnp.log1p(jnp.exp(-jnp.abs(x)))


def _log_sigmoid(x):
    return jnp.minimum(x, 0.0) - jnp.log1p(jnp.exp(-jnp.abs(x)))


def _split3(x):
    hi = x.astype(BF16)
    r = x - hi.astype(F32)
    mid = r.astype(BF16)
    lo = (r - mid.astype(F32)).astype(BF16)
    return hi, mid, lo


def _dot3_left(m, x):
    hi, mid, lo = _split3(x)
    d = lambda a: jnp.dot(m, a, preferred_element_type=F32)
    return d(lo) + d(mid) + d(hi)


def _dot3_right(x, m):
    hi, mid, lo = _split3(x)
    d = lambda a: jnp.dot(a, m, preferred_element_type=F32)
    return d(lo) + d(mid) + d(hi)


def _dot_nt(a, b):
    return lax.dot_general(a, b, (((1,), (1,)), ((), ())), preferred_element_type=F32)


def _dot_tn(a, b):
    return lax.dot_general(a, b, (((0,), (0,)), ((), ())), preferred_element_type=F32)


def _norm_mm_kernel(*refs, hn_blocks, tn, has_side):
    if has_side:
        x_ref, nw_ref, w_ref, hn_ref, w2_ref, o_ref, o2_ref, h_ref = refs
    else:
        x_ref, nw_ref, w_ref, hn_ref, o_ref, h_ref = refs
    j = pl.program_id(1)

    @pl.when(j == 0)
    def _():
        x = x_ref[...]
        ms = jnp.mean(x * x, axis=-1, keepdims=True)
        h_ref[...] = ((x * lax.rsqrt(ms + EPS)) * nw_ref[...]).astype(BF16)
        if has_side:
            o2_ref[...] = jnp.dot(h_ref[...], w2_ref[...], preferred_element_type=F32)

    acc = jnp.dot(h_ref[...], w_ref[...], preferred_element_type=F32)
    if hn_blocks == 0:
        o_ref[...] = acc
    else:
        @pl.when(j >= hn_blocks)
        def _():
            o_ref[...] = acc

        @pl.when(j < hn_blocks)
        def _():
            for s in range(tn // LANES):
                cs = slice(s * LANES, (s + 1) * LANES)
                seg = acc[:, cs]
                ms = jnp.mean(seg * seg, axis=-1, keepdims=True)
                o_ref[:, cs] = (seg * lax.rsqrt(ms + EPS)) * hn_ref[:, cs]


def _norm_mm(x, nw, w, hn=None, hn_cols=0, w_side=None, n_cols=None):
    m, d = x.shape
    n = n_cols or w.shape[1]
    tm = min(m, 1024)
    tn = min(n, 1024)
    assert m % tm == 0 and n % tn == 0 and hn_cols % tn == 0
    if hn is None:
        hn = jnp.ones((1, n), F32)
    has_side = w_side is not None
    in_specs = [
        pl.BlockSpec((tm, d), lambda i, j: (i, 0)),
        pl.BlockSpec((1, d), lambda i, j: (0, 0)),
        pl.BlockSpec((d, tn), lambda i, j: (0, j)),
        pl.BlockSpec((1, tn), lambda i, j: (0, j)),
    ]
    out_specs = [pl.BlockSpec((tm, tn), lambda i, j: (i, j))]
    out_shape = [jax.ShapeDtypeStruct((m, n), F32)]
    args = [x, nw.reshape(1, d), w, hn]
    if has_side:
        in_specs.append(pl.BlockSpec((d, LANES), lambda i, j: (0, 0)))
        out_specs.append(pl.BlockSpec((tm, LANES), lambda i, j: (i, 0)))
        out_shape.append(jax.ShapeDtypeStruct((m, LANES), F32))
        args.append(w_side)
    outs = pl.pallas_call(
        functools.partial(_norm_mm_kernel, hn_blocks=hn_cols // tn, tn=tn, has_side=has_side),
        grid=(m // tm, n // tn),
        in_specs=in_specs,
        out_specs=out_specs,
        out_shape=out_shape,
        scratch_shapes=[pltpu.VMEM((tm, d), BF16)],
        compiler_params=pltpu.CompilerParams(dimension_semantics=("arbitrary", "arbitrary")),
        name="norm_mm",
    )(*args)
    return outs if has_side else outs[0]


def _mm_res_kernel(l_ref, w_ref, r_ref, o_ref):
    o_ref[...] = r_ref[...] + jnp.dot(l_ref[...], w_ref[...], preferred_element_type=F32)


def _mm_res(lhs, w, res):
    m, k = lhs.shape
    n = w.shape[1]
    tm = min(m, 512)
    tn = min(n, 1024)
    assert m % tm == 0 and n % tn == 0
    return pl.pallas_call(
        _mm_res_kernel,
        grid=(m // tm, n // tn),
        in_specs=[
            pl.BlockSpec((tm, k), lambda i, j: (i, 0)),
            pl.BlockSpec((k, tn), lambda i, j: (0, j)),
            pl.BlockSpec((tm, tn), lambda i, j: (i, j)),
        ],
        out_specs=pl.BlockSpec((tm, tn), lambda i, j: (i, j)),
        out_shape=jax.ShapeDtypeStruct((m, n), F32),
        compiler_params=pltpu.CompilerParams(dimension_semantics=("arbitrary", "arbitrary")),
        name="mm_res",
    )(lhs, w, res)


def _ssd_kernel(z_ref, *refs, s_true, n_chunks):
    n_xbc = SSD_CONV_DIM // XBC_BLOCK
    xbc_refs = refs[:n_xbc]
    (dt_ref, cw_ref, cb_ref, dtb_ref, alog_ref, dexp_ref, nw_ref, cinit_ref, sinit_ref,
     y_ref, hout_ref, ext_ref, act_ref, yacc_ref, ht_ref) = refs[n_xbc:]
    c = pl.program_id(1)
    L = CHUNK
    n_blk = D_INNER // LANES

    @pl.when(c == 0)
    def _():
        ext_ref[0:8, :] = cinit_ref[0]
        for i in range(n_blk):
            cs = slice(i * LANES, (i + 1) * LANES)
            ht_ref[:, cs] = sinit_ref[0, cs, :].T

    for k in range(n_xbc):
        ext_ref[8:8 + L, k * XBC_BLOCK:(k + 1) * XBC_BLOCK] = xbc_refs[k][0]
    slab = 512
    for s in range(SSD_CONV_DIM // slab):
        cs = slice(s * slab, (s + 1) * slab)
        acc = ext_ref[5:5 + L, cs] * cw_ref[0:1, cs]
        for k in range(1, SSD_D_CONV):
            acc = acc + ext_ref[5 + k:5 + k + L, cs] * cw_ref[k:k + 1, cs]
        act_ref[:, cs] = _silu(acc + cb_ref[:, cs])
    ext_ref[0:8, :] = ext_ref[L:L + 8, :]

    row = lax.broadcasted_iota(jnp.int32, (L, LANES), 0)
    col = lax.broadcasted_iota(jnp.int32, (L, LANES), 1)
    tril = row >= col
    lo = col < SSD_HEAD_DIM
    dtv = _softplus(dt_ref[0] + dtb_ref[...])
    dtv = jnp.where(row + c * L < s_true, dtv, 0.0)
    a = dtv * (-jnp.exp(alog_ref[...]))
    tri = jnp.where(tril, 1.0, 0.0).astype(BF16)
    cum = _dot3_left(tri, a)
    cum_t = cum.T
    dt_t = dtv.T
    ecum = jnp.exp(cum)
    clast = cum[L - 1:L, :]
    tailw = jnp.exp(clast - cum) * dtv
    elast = jnp.exp(clast)

    x_off = 0
    b_off = D_INNER
    c_off = D_INNER + SSD_N_GROUPS * SSD_D_STATE
    heads_per_group = SSD_N_HEADS // SSD_N_GROUPS
    gw = heads_per_group * SSD_HEAD_DIM
    for g in range(SSD_N_GROUPS):
        bb = act_ref[:, b_off + g * SSD_D_STATE:b_off + (g + 1) * SSD_D_STATE].astype(BF16)
        cc = act_ref[:, c_off + g * SSD_D_STATE:c_off + (g + 1) * SSD_D_STATE].astype(BF16)
        cb = _dot_nt(cc, bb)
        gs = slice(g * gw, (g + 1) * gw)
        h_old = ht_ref[:, gs]
        y_state = jnp.dot(cc, h_old.astype(BF16), preferred_element_type=F32)
        xw_parts = []
        el_parts = []
        for i in range(heads_per_group // 2):
            h0 = g * heads_per_group + 2 * i
            h1 = h0 + 1
            pc = slice(x_off + h0 * SSD_HEAD_DIM, x_off + h0 * SSD_HEAD_DIM + LANES)
            xp = act_ref[:, pc]

            def head_mix(h):
                seg = cum[:, h:h + 1] - cum_t[h:h + 1, :]
                return jnp.exp(jnp.where(tril, seg, -jnp.inf)) * cb * dt_t[h:h + 1, :]

            m_cat = jnp.concatenate([head_mix(h0), head_mix(h1)], axis=1).astype(BF16)
            x_bd = jnp.concatenate([jnp.where(lo, xp, 0.0), jnp.where(lo, 0.0, xp)], axis=0).astype(BF16)
            yp = jnp.dot(m_cat, x_bd, preferred_element_type=F32)
            ep = jnp.where(lo, ecum[:, h0:h0 + 1], ecum[:, h1:h1 + 1])
            yp = yp + y_state[:, i * LANES:(i + 1) * LANES] * ep
            yp = yp + dexp_ref[:, pc] * xp
            yacc_ref[:, pc] = yp
            tw = jnp.where(lo, tailw[:, h0:h0 + 1], tailw[:, h1:h1 + 1])
            xw_parts.append((xp * tw).astype(BF16))
            el_parts.append(jnp.where(lo[0:1, :], elast[:, h0:h0 + 1], elast[:, h1:h1 + 1]))
        xw = jnp.concatenate(xw_parts, axis=1)
        el = jnp.concatenate(el_parts, axis=1)
        ht_ref[:, gs] = h_old * el + _dot_tn(bb, xw)

    gated = yacc_ref[...] * _silu(z_ref[0])
    ms = jnp.mean(gated * gated, axis=-1, keepdims=True)
    y_ref[0] = ((gated * lax.rsqrt(ms + EPS)) * nw_ref[...]).astype(BF16)

    @pl.when(c == n_chunks - 1)
    def _():
        for i in range(n_blk):
            cs = slice(i * LANES, (i + 1) * LANES)
            hout_ref[0, cs, :] = ht_ref[:, cs].T


def _ssd_core(zx, dtr, conv_w, conv_b, dt_bias, a_log, d_skip, norm_w, conv_init, ssm_init, s_true):
    b, s, _ = zx.shape
    nc = s // CHUNK
    hp = SSD_N_HEADS * SSD_HEAD_DIM
    pad_h = LANES - SSD_N_HEADS
    const = lambda *shape: pl.BlockSpec(shape, lambda bi, ci: (0,) * len(shape))
    n_xbc = SSD_CONV_DIM // XBC_BLOCK
    xbc_spec = lambda k: pl.BlockSpec((1, CHUNK, XBC_BLOCK),
                                      lambda bi, ci, k=k: (bi, ci, D_INNER // XBC_BLOCK + k))
    return pl.pallas_call(
        functools.partial(_ssd_kernel, s_true=s_true, n_chunks=nc),
        grid=(b, nc),
        in_specs=[
            pl.BlockSpec((1, CHUNK, D_INNER), lambda bi, ci: (bi, ci, 0)),
            *[xbc_spec(k) for k in range(n_xbc)],
            pl.BlockSpec((1, CHUNK, LANES), lambda bi, ci: (bi, ci, 0)),
            const(SSD_D_CONV, SSD_CONV_DIM),
            const(1, SSD_CONV_DIM),
            const(1, LANES),
            const(1, LANES),
            const(1, D_INNER),
            const(1, D_INNER),
            pl.BlockSpec((1, 8, SSD_CONV_DIM), lambda bi, ci: (bi, 0, 0)),
            pl.BlockSpec((1, hp, SSD_D_STATE), lambda bi, ci: (bi, 0, 0)),
        ],
        out_specs=[
            pl.BlockSpec((1, CHUNK, D_INNER), lambda bi, ci: (bi, ci, 0)),
            pl.BlockSpec((1, hp, SSD_D_STATE), lambda bi, ci: (bi, 0, 0)),
        ],
        out_shape=[
            jax.ShapeDtypeStruct((b, s, D_INNER), BF16),
            jax.ShapeDtypeStruct((b, hp, SSD_D_STATE), F32),
        ],
        scratch_shapes=[
            pltpu.VMEM((CHUNK + 8, SSD_CONV_DIM), F32),
            pltpu.VMEM((CHUNK, SSD_CONV_DIM), F32),
            pltpu.VMEM((CHUNK, D_INNER), F32),
            pltpu.VMEM((SSD_D_STATE, hp), F32),
        ],
        compiler_params=pltpu.CompilerParams(dimension_semantics=("arbitrary", "arbitrary")),
        name="ssd_core",
    )(zx, *([zx] * n_xbc), dtr, conv_w, conv_b.reshape(1, -1),
      jnp.pad(dt_bias, (0, pad_h)).reshape(1, LANES), jnp.pad(a_log, (0, pad_h)).reshape(1, LANES),
      jnp.repeat(d_skip, SSD_HEAD_DIM).reshape(1, hp), norm_w.reshape(1, -1), conv_init, ssm_init)


def _pool_kernel(p_ref, z_ref, hinit_ref, wg_ref, sc_ref, o_ref, ext_ref, *, tm, pos0):
    t = pl.program_id(1)
    hist_rows = POOL_HIST + 1

    @pl.when(t == 0)
    def _():
        ext_ref[0:hist_rows, :] = hinit_ref[0]

    ext_ref[hist_rows:hist_rows + tm, :] = p_ref[0]
    pos = lax.broadcasted_iota(jnp.int32, (tm, POOL_GROUP_DIM), 0) + t * tm + pos0
    for g, w in enumerate(POOL_WINDOWS):
        cs = slice(g * POOL_GROUP_DIM, (g + 1) * POOL_GROUP_DIM)
        win = ext_ref[hist_rows:hist_rows + tm, cs]
        for j in range(1, w):
            win = win + ext_ref[hist_rows - j:hist_rows - j + tm, cs]
        count = jnp.minimum(pos + 1, w).astype(F32)
        diff = (win / count - p_ref[0, :, cs]).astype(BF16)
        mixed = jnp.dot(diff, wg_ref[g], preferred_element_type=F32) * sc_ref[:, cs]
        o_ref[0, :, cs] = (mixed * _silu(z_ref[0, :, cs])).astype(BF16)
    ext_ref[0:hist_rows, :] = ext_ref[tm:tm + hist_rows, :]


def _pool_core(pz, hist_init, w_grp, scale, pos0):
    b, s, _ = pz.shape
    tm = min(s, 256)
    return pl.pallas_call(
        functools.partial(_pool_kernel, tm=tm, pos0=pos0),
        grid=(b, s // tm),
        in_specs=[
            pl.BlockSpec((1, tm, D_INNER), lambda bi, ti: (bi, ti, 0)),
            pl.BlockSpec((1, tm, D_INNER), lambda bi, ti: (bi, ti, 1)),
            pl.BlockSpec((1, POOL_HIST + 1, D_INNER), lambda bi, ti: (bi, 0, 0)),
            pl.BlockSpec((len(POOL_WINDOWS), POOL_GROUP_DIM, POOL_GROUP_DIM), lambda bi, ti: (0, 0, 0)),
            pl.BlockSpec((1, D_INNER), lambda bi, ti: (0, 0)),
        ],
        out_specs=pl.BlockSpec((1, tm, D_INNER), lambda bi, ti: (bi, ti, 0)),
        out_shape=jax.ShapeDtypeStruct((b, s, D_INNER), BF16),
        scratch_shapes=[pltpu.VMEM((tm + POOL_HIST + 1, D_INNER), F32)],
        compiler_params=pltpu.CompilerParams(dimension_semantics=("arbitrary", "arbitrary")),
        name="pool_core",
    )(pz, pz, hist_init, w_grp, scale.reshape(1, -1))


def _sgu_kernel(u_ref, v_ref, z_ref, vnw_ref, ws_ref, bst_ref, g_ref, vout_ref, *, n_chunks):
    c = pl.program_id(1)
    v = v_ref[0]
    ms = jnp.mean(v * v, axis=-1, keepdims=True)
    vn = (v * lax.rsqrt(ms + EPS)) * vnw_ref[...]
    row = lax.broadcasted_iota(jnp.int32, (CHUNK, CHUNK), 0)
    col = lax.broadcasted_iota(jnp.int32, (CHUNK, CHUNK), 1)
    tril = row >= col
    for g in range(SGU_N_GROUPS):
        cs = slice(g * SGU_GROUP_DIM, (g + 1) * SGU_GROUP_DIM)
        w = jnp.where(tril, ws_ref[g], 0.0).astype(BF16)
        mixed = jnp.dot(w, vn[:, cs].astype(BF16), preferred_element_type=F32) + bst_ref[:, g:g + 1]
        g_ref[0, :, cs] = ((u_ref[0, :, cs] * mixed) * _silu(z_ref[0, :, cs])).astype(BF16)

    @pl.when(c == n_chunks - 1)
    def _():
        vout_ref[0] = vn


def _sgu_core(uvz, v_norm, w_s, b_s):
    b, s, _ = uvz.shape
    nc = s // CHUNK
    blk = lambda k: pl.BlockSpec((1, CHUNK, D_INNER), lambda bi, ci, k=k: (bi, ci, k))
    return pl.pallas_call(
        functools.partial(_sgu_kernel, n_chunks=nc),
        grid=(b, nc),
        in_specs=[
            blk(0), blk(1), blk(2),
            pl.BlockSpec((1, D_INNER), lambda bi, ci: (0, 0)),
            pl.BlockSpec((SGU_N_GROUPS, CHUNK, CHUNK), lambda bi, ci: (0, 0, 0)),
            pl.BlockSpec((CHUNK, SGU_N_GROUPS), lambda bi, ci: (0, 0)),
        ],
        out_specs=[
            pl.BlockSpec((1, CHUNK, D_INNER), lambda bi, ci: (bi, ci, 0)),
            pl.BlockSpec((1, CHUNK, D_INNER), lambda bi, ci: (bi, 0, 0)),
        ],
        out_shape=[
            jax.ShapeDtypeStruct((b, s, D_INNER), BF16),
            jax.ShapeDtypeStruct((b, CHUNK, D_INNER), F32),
        ],
        compiler_params=pltpu.CompilerParams(dimension_semantics=("arbitrary", "arbitrary")),
        name="sgu_core",
    )(uvz, uvz, uvz, v_norm.reshape(1, -1), w_s, jnp.transpose(b_s))


def _fox_gate_kernel(fl_ref, bf_ref, lf_ref, ft_ref, carry_ref, *, tt):
    t = pl.program_id(1)

    @pl.when(t == 0)
    def _():
        carry_ref[...] = jnp.zeros_like(carry_ref)

    lf = _log_sigmoid(fl_ref[0] + bf_ref[...])
    row = lax.broadcasted_iota(jnp.int32, (tt, tt), 0)
    col = lax.broadcasted_iota(jnp.int32, (tt, tt), 1)
    tri = jnp.where(row >= col, 1.0, 0.0).astype(BF16)
    cum = _dot3_left(tri, lf) + carry_ref[0:1, :]
    carry_ref[0:1, :] = cum[tt - 1:tt, :]
    lf_ref[0] = lf
    ft_ref[0] = cum.T


def _fox_gates(f_logit, b_f):
    b, s, _ = f_logit.shape
    tt = min(s, 256)
    return pl.pallas_call(
        functools.partial(_fox_gate_kernel, tt=tt),
        grid=(b, s // tt),
        in_specs=[
            pl.BlockSpec((1, tt, LANES), lambda bi, ti: (bi, ti, 0)),
            pl.BlockSpec((1, LANES), lambda bi, ti: (0, 0)),
        ],
        out_specs=[
            pl.BlockSpec((1, tt, LANES), lambda bi, ti: (bi, ti, 0)),
            pl.BlockSpec((1, LANES, tt), lambda bi, ti: (bi, 0, ti)),
        ],
        out_shape=[
            jax.ShapeDtypeStruct((b, s, LANES), F32),
            jax.ShapeDtypeStruct((b, LANES, s), F32),
        ],
        scratch_shapes=[pltpu.VMEM((8, LANES), F32)],
        compiler_params=pltpu.CompilerParams(dimension_semantics=("arbitrary", "arbitrary")),
        name="fox_gates",
    )(f_logit, b_f)


def _fox_attn_kernel(q_ref, k_ref, v_ref, z_ref, ft_ref, o_ref, m_ref, l_ref, acc_ref, *, tq, tk):
    qi = pl.program_id(1)
    ki = pl.program_id(2)
    log2e = 1.4426950408889634
    q_scale = FOX_HEAD_DIM ** -0.5 * log2e

    @pl.when(ki == 0)
    def _():
        m_ref[...] = jnp.full_like(m_ref, -jnp.inf)
        l_ref[...] = jnp.zeros_like(l_ref)
        acc_ref[...] = jnp.zeros_like(acc_ref)

    def step(on_diagonal):
        if on_diagonal:
            mask = (lax.broadcasted_iota(jnp.int32, (tq, tk), 1)
                    <= lax.broadcasted_iota(jnp.int32, (tq, tk), 0))
        ones = jnp.ones((tk, LANES), BF16)
        for h in range(FOX_N_HEADS):
            hs = slice(h * FOX_HEAD_DIM, (h + 1) * FOX_HEAD_DIM)
            q = (q_ref[0, :, hs] * q_scale).astype(BF16)
            k = k_ref[0, :, hs].astype(BF16)
            s = _dot_nt(q, k) - ft_ref[0, h:h + 1, :] * log2e
            if on_diagonal:
                s = jnp.where(mask, s, -jnp.inf)
            m_prev = m_ref[h]
            m_new = jnp.maximum(m_prev, jnp.max(s, axis=-1, keepdims=True))
            alpha = jnp.exp2(m_prev - m_new)
            p = jnp.exp2(s - m_new[:, 0:1]).astype(BF16)
            v1 = jnp.concatenate([v_ref[0, :, hs].astype(BF16), ones], axis=1)
            pv = jnp.dot(p, v1, preferred_element_type=F32)
            acc_ref[:, hs] = alpha * acc_ref[:, hs] + pv[:, :FOX_HEAD_DIM]
            l_ref[h] = alpha * l_ref[h] + pv[:, FOX_HEAD_DIM:]
            m_ref[h] = m_new

    @pl.when(ki < qi)
    def _():
        step(False)

    @pl.when(ki == qi)
    def _():
        step(True)
        for h in range(FOX_N_HEADS):
            hs = slice(h * FOX_HEAD_DIM, (h + 1) * FOX_HEAD_DIM)
            a = acc_ref[:, hs] / l_ref[h]
            o_ref[0, :, hs] = (a * _silu(z_ref[0, :, hs])).astype(BF16)


def _fox_attn(qkvz, ft):
    b, s, _ = qkvz.shape
    tq = tk = min(s, 512)
    nq = s // tq
    kv = lambda k: pl.BlockSpec((1, tk, FOX_WIDTH), lambda bi, qi, ki, k=k: (bi, jnp.minimum(ki, qi), k))
    qz = lambda k: pl.BlockSpec((1, tq, FOX_WIDTH), lambda bi, qi, ki, k=k: (bi, qi, k))
    return pl.pallas_call(
        functools.partial(_fox_attn_kernel, tq=tq, tk=tk),
        grid=(b, nq, nq),
        in_specs=[
            qz(0), kv(1), kv(2), qz(3),
            pl.BlockSpec((1, LANES, tk), lambda bi, qi, ki: (bi, 0, jnp.minimum(ki, qi))),
        ],
        out_specs=pl.BlockSpec((1, tq, FOX_WIDTH), lambda bi, qi, ki: (bi, qi, 0)),
        out_shape=jax.ShapeDtypeStruct((b, s, FOX_WIDTH), BF16),
        scratch_shapes=[
            pltpu.VMEM((FOX_N_HEADS, tq, LANES), F32),
            pltpu.VMEM((FOX_N_HEADS, tq, LANES), F32),
            pltpu.VMEM((tq, FOX_WIDTH), F32),
        ],
        compiler_params=pltpu.CompilerParams(dimension_semantics=("arbitrary", "arbitrary", "arbitrary")),
        name="fox_attn",
    )(qkvz, qkvz, qkvz, qkvz, ft)


def _fox_decode_kernel(pt_ref, *refs, n_pages):
    pps = PAGES_PER_STEP
    k_refs = refs[0:pps]
    v_refs = refs[pps:2 * pps]
    lf_refs = refs[2 * pps:3 * pps]
    (q_ref, kn_ref, vn_ref, z_ref, fl_ref, bf_ref, o_ref, lfn_ref,
     m_ref, l_ref, acc_ref, carry_ref) = refs[3 * pps:]
    del pt_ref
    j = pl.program_id(1)
    n_steps = n_pages // pps
    nh = FOX_N_HEADS
    scale = FOX_HEAD_DIM ** -0.5
    hrow = lax.broadcasted_iota(jnp.int32, (nh, LANES), 0)
    lane = lax.broadcasted_iota(jnp.int32, (nh, LANES), 1)
    own = (lane % nh) == hrow

    @pl.when(j == 0)
    def _():
        lfn = _log_sigmoid(fl_ref[0] + bf_ref[...])
        lfn_ref[0] = lfn
        fn_col = jnp.sum(jnp.where(lane == hrow, lfn, 0.0), axis=-1, keepdims=True)
        s_new = jnp.sum(q_ref[0] * kn_ref[0], axis=-1, keepdims=True) * scale - fn_col
        m_ref[...] = jnp.broadcast_to(s_new, (nh, LANES))
        l_ref[...] = jnp.ones_like(l_ref)
        acc_ref[...] = vn_ref[0]
        carry_ref[...] = jnp.zeros_like(carry_ref)

    rows_pp = PAGE_SIZE * nh // LANES
    n_rows = pps * rows_pp
    x4 = jnp.concatenate([lf_refs[i][0] for i in range(pps)], axis=0)
    src = lax.broadcasted_iota(jnp.int32, (LANES, 2 * LANES), 0)
    dst = lax.broadcasted_iota(jnp.int32, (LANES, 2 * LANES), 1)
    same_head = (src % nh) == (dst % nh)
    w12 = jnp.where(same_head & ((dst >= LANES) | (src > dst)), 1.0, 0.0).astype(BF16)
    r12 = _dot3_right(x4, w12)
    within = r12[:, :LANES]
    tot = r12[:, LANES:]
    rr = lax.broadcasted_iota(jnp.int32, (n_rows, LANES), 0)
    cc = lax.broadcasted_iota(jnp.int32, (n_rows, LANES), 1)
    later = (cc < n_rows) & ((cc // rows_pp < rr // rows_pp) | ((cc // rows_pp == rr // rows_pp) & (cc > rr)))
    tot_pad = jnp.concatenate([tot, jnp.zeros((LANES - n_rows, LANES), F32)], axis=0)
    carry = carry_ref[0:1, :]
    tail4 = within + _dot3_left(jnp.where(later, 1.0, 0.0).astype(BF16), tot_pad) + carry
    carry_ref[0:1, :] = carry + jnp.sum(tot, axis=0, keepdims=True)

    qb = q_ref[0].astype(BF16)
    scores = []
    for i in range(pps):
        s = _dot_nt(qb, k_refs[i][0].astype(BF16)) * scale
        parts = []
        for a in range(rows_pp):
            r = i * rows_pp + a
            parts.append(jnp.where(own, s[:, a * LANES:(a + 1) * LANES] + tail4[r:r + 1, :], -jnp.inf))
        scores.append(jnp.concatenate(parts, axis=1))
    m_prev = m_ref[:, 0:1]
    m_new = m_prev
    for s in scores:
        m_new = jnp.maximum(m_new, jnp.max(s, axis=-1, keepdims=True))
    alpha = jnp.exp(m_prev - m_new)
    l_new = alpha * l_ref[:, 0:1]
    acc = alpha * acc_ref[...]
    for i, s in enumerate(scores):
        p = jnp.exp(s - m_new)
        l_new = l_new + jnp.sum(p, axis=-1, keepdims=True)
        acc = acc + jnp.dot(p.astype(BF16), v_refs[i][0].astype(BF16), preferred_element_type=F32)
    m_ref[...] = jnp.broadcast_to(m_new, (nh, LANES))
    l_ref[...] = jnp.broadcast_to(l_new, (nh, LANES))
    acc_ref[...] = acc

    @pl.when(j == n_steps - 1)
    def _():
        o_ref[0] = (acc / l_new) * _silu(z_ref[0])


def _fox_decode(q, k_new, v_new, z, f_logit, b_f, cache_k, cache_v, cache_lf, page_table):
    b = q.shape[0]
    n_pages = page_table.shape[1]
    pps = PAGES_PER_STEP
    assert n_pages % pps == 0
    n_steps = n_pages // pps
    keys = PAGE_SIZE * FOX_N_HEADS

    def page_map(i):
        return lambda bi, j, pt: (pt[bi * n_pages + (n_pages - 1 - (pps * j + i))], 0, 0)

    kv_spec = lambda i: pl.BlockSpec((1, keys, FOX_HEAD_DIM), page_map(i))
    lf_spec = lambda i: pl.BlockSpec((1, keys // LANES, LANES), page_map(i))
    tok = lambda r: pl.BlockSpec((1, r, LANES), lambda bi, j, pt: (bi, 0, 0))
    grid_spec = pltpu.PrefetchScalarGridSpec(
        num_scalar_prefetch=1,
        grid=(b, n_steps),
        in_specs=([kv_spec(i) for i in range(pps)] + [kv_spec(i) for i in range(pps)]
                  + [lf_spec(i) for i in range(pps)]
                  + [tok(FOX_N_HEADS), tok(FOX_N_HEADS), tok(FOX_N_HEADS), tok(FOX_N_HEADS), tok(1),
                     pl.BlockSpec((1, LANES), lambda bi, j, pt: (0, 0))]),
        out_specs=[tok(FOX_N_HEADS), tok(1)],
        scratch_shapes=[
            pltpu.VMEM((FOX_N_HEADS, LANES), F32),
            pltpu.VMEM((FOX_N_HEADS, LANES), F32),
            pltpu.VMEM((FOX_N_HEADS, FOX_HEAD_DIM), F32),
            pltpu.VMEM((8, LANES), F32),
        ],
    )
    out, lfn = pl.pallas_call(
        functools.partial(_fox_decode_kernel, n_pages=n_pages),
        grid_spec=grid_spec,
        out_shape=[jax.ShapeDtypeStruct((b, FOX_N_HEADS, FOX_HEAD_DIM), F32),
                   jax.ShapeDtypeStruct((b, 1, LANES), F32)],
        compiler_params=pltpu.CompilerParams(dimension_semantics=("arbitrary", "arbitrary")),
        name="fox_decode",
    )(page_table.reshape(-1), *([cache_k] * pps), *([cache_v] * pps), *([cache_lf] * pps),
      q, k_new, v_new, z, f_logit.reshape(b, 1, LANES), b_f)
    return out, lfn.reshape(b, LANES)


def _pad_rows(a, rows):
    return jnp.pad(a, ((0, rows - a.shape[0]),) + ((0, 0),) * (a.ndim - 1))


def _pad_cols(a, cols):
    return jnp.pad(a, ((0, 0),) * (a.ndim - 1) + ((0, cols - a.shape[-1]),))


def _run(x, s_true, pos0, conv_buf, ssm_state, pool_hist, attend, wts):
    b = x.shape[0]
    m_true = b * s_true
    m = max(m_true, ROW_PAD)
    s_pad = max(s_true, CHUNK)
    xf = _pad_rows(x.reshape(m_true, D_MODEL), m)

    def seq(a):
        a = a[:m_true].reshape(b, s_true, a.shape[-1])
        return jnp.pad(a, ((0, 0), (0, s_pad - s_true), (0, 0)))

    def rows(a):
        return _pad_rows(a[:, :s_true].reshape(m_true, a.shape[-1]), m)

    zx, dtr = _norm_mm(xf, wts['norm_w'][0], wts['ssd_w_in'], w_side=wts['ssd_w_dt'],
                       n_cols=D_INNER + SSD_CONV_DIM)
    zx_s = seq(zx)
    conv_init = jnp.pad(conv_buf, ((0, 0), (8 - (SSD_D_CONV - 1), 0), (0, 0)))
    y, h_new = _ssd_core(zx_s, seq(dtr), wts['ssd_conv_w'], wts['ssd_conv_b'], wts['ssd_dt_bias'],
                         wts['ssd_A_log'], wts['ssd_D'], wts['ssd_norm_w'], conv_init,
                         ssm_state.reshape(b, SSD_N_HEADS * SSD_HEAD_DIM, SSD_D_STATE), s_true)
    xf = _mm_res(rows(y), wts['ssd_w_out'], xf)
    n_keep = min(s_true, SSD_D_CONV - 1)
    conv_o = jnp.concatenate([conv_buf, zx_s[:, s_true - n_keep:s_true, D_INNER:]], axis=1)[:, -(SSD_D_CONV - 1):]
    ssm_o = h_new.reshape(b, SSD_N_HEADS, SSD_HEAD_DIM, SSD_D_STATE)

    pz = seq(_norm_mm(xf, wts['norm_w'][1], wts['pool_w_in']))
    hist_init = jnp.pad(pool_hist, ((0, 0), (1, 0), (0, 0)))
    g = _pool_core(pz, hist_init, wts['pool_w_grp'], wts['pool_scale'], pos0)
    xf = _mm_res(rows(g), wts['pool_w_out'], xf)
    pool_o = jnp.concatenate([pool_hist, pz[:, :s_true, :D_INNER]], axis=1)[:, -POOL_HIST:]

    qkvz, f_logit = _norm_mm(xf, wts['norm_w'][2], wts['fox_w_in'], hn=wts['fox_hn'], hn_cols=2 * FOX_WIDTH,
                             w_side=wts['fox_w_f'], n_cols=4 * FOX_WIDTH)
    g, logf = attend(qkvz[:m_true], f_logit[:m_true])
    xf = _mm_res(_pad_rows(g, m), wts['fox_w_out'], xf)
    k_o = qkvz[:m_true, FOX_WIDTH:2 * FOX_WIDTH].reshape(b, s_true, FOX_N_HEADS, FOX_HEAD_DIM)
    v_o = qkvz[:m_true, 2 * FOX_WIDTH:3 * FOX_WIDTH].reshape(b, s_true, FOX_N_HEADS, FOX_HEAD_DIM)
    lf_o = logf[:, :FOX_N_HEADS].reshape(b, s_true, FOX_N_HEADS)

    uvz = seq(_norm_mm(xf, wts['norm_w'][3], wts['sgu_w_in']))
    g, v_last = _sgu_core(uvz, wts['sgu_v_norm'], wts['sgu_w_s'], wts['sgu_b_s'])
    xf = _mm_res(rows(g), wts['sgu_w_out'], xf)
    n_last = s_true - ((s_true - 1) // CHUNK) * CHUNK
    sgu_o = v_last[:, :n_last]

    y_out = xf[:m_true].reshape(b, s_true, D_MODEL)
    return (y_out, conv_o[None], ssm_o[None], pool_o[None], k_o[None], v_o[None], lf_o[None], sgu_o[None])


def kernel(x_prompt, x_sample, state_ssd_conv, state_ssd_ssm, state_pool, cache_fox_k, cache_fox_v, cache_fox_logf, page_table, norm_w, ssd_w_in, ssd_conv_w, ssd_conv_b, ssd_dt_bias, ssd_A_log, ssd_D, ssd_norm_w, ssd_w_out, pool_w_in, pool_w_grp, pool_scale, pool_w_out, fox_w_in, fox_b_f, fox_q_norm, fox_k_norm, fox_w_out, sgu_w_in, sgu_v_norm, sgu_w_s, sgu_b_s, sgu_w_out):
    assert norm_w.shape[0] == 4, "one layer of each mixer kind"
    bf = lambda a: a.astype(BF16)
    w_in = ssd_w_in[0]
    fox_in = fox_w_in[0]
    wts = {
        'norm_w': norm_w,
        'ssd_w_in': bf(w_in),
        'ssd_w_dt': bf(_pad_cols(w_in[:, D_INNER + SSD_CONV_DIM:], LANES)),
        'ssd_conv_w': ssd_conv_w[0], 'ssd_conv_b': ssd_conv_b[0], 'ssd_dt_bias': ssd_dt_bias[0],
        'ssd_A_log': ssd_A_log[0], 'ssd_D': ssd_D[0], 'ssd_norm_w': ssd_norm_w[0],
        'ssd_w_out': bf(ssd_w_out[0]),
        'pool_w_in': bf(pool_w_in[0]), 'pool_w_grp': bf(pool_w_grp[0]), 'pool_scale': pool_scale[0],
        'pool_w_out': bf(pool_w_out[0]),
        'fox_w_in': bf(fox_in),
        'fox_w_f': bf(_pad_cols(fox_in[:, 4 * FOX_WIDTH:], LANES)),
        'fox_hn': jnp.concatenate([jnp.tile(fox_q_norm[0], FOX_N_HEADS), jnp.tile(fox_k_norm[0], FOX_N_HEADS),
                                   jnp.ones((2 * FOX_WIDTH,), F32)]).reshape(1, 4 * FOX_WIDTH),
        'fox_w_out': bf(fox_w_out[0]),
        'sgu_w_in': bf(sgu_w_in[0]), 'sgu_v_norm': sgu_v_norm[0], 'sgu_w_s': sgu_w_s[0], 'sgu_b_s': sgu_b_s[0],
        'sgu_w_out': bf(sgu_w_out[0]),
    }
    b_f = _pad_cols(fox_b_f[0].reshape(1, FOX_N_HEADS), LANES)

    bp, sp, _ = x_prompt.shape
    bs = x_sample.shape[0]
    assert x_sample.shape[1] == 1, "the cached group decodes one token per sequence"

    def attend_prompt(qkvz, f_logit):
        lf, ft = _fox_gates(f_logit.reshape(bp, sp, LANES), b_f)
        g = _fox_attn(qkvz.reshape(bp, sp, 4 * FOX_WIDTH), ft)
        return g.reshape(bp * sp, FOX_WIDTH), lf.reshape(bp * sp, LANES)

    n_phys = cache_fox_k.shape[1]
    keys = PAGE_SIZE * FOX_N_HEADS
    ck = cache_fox_k[0].reshape(n_phys, keys, FOX_HEAD_DIM)
    cv = cache_fox_v[0].reshape(n_phys, keys, FOX_HEAD_DIM)
    clf = cache_fox_logf[0].reshape(n_phys, keys // LANES, LANES)

    def attend_sample(qkvz, f_logit):
        q, k, v, z = (qkvz[:, i * FOX_WIDTH:(i + 1) * FOX_WIDTH].reshape(bs, FOX_N_HEADS, FOX_HEAD_DIM)
                      for i in range(4))
        g, lf = _fox_decode(q, k, v, z, f_logit, b_f, ck, cv, clf, page_table)
        return g.reshape(bs, FOX_WIDTH).astype(BF16), lf

    zero_conv = jnp.zeros((bp, SSD_D_CONV - 1, SSD_CONV_DIM), F32)
    zero_ssm = jnp.zeros((bp, SSD_N_HEADS, SSD_HEAD_DIM, SSD_D_STATE), F32)
    zero_pool = jnp.zeros((bp, POOL_HIST, D_INNER), F32)
    past_len = page_table.shape[1] * PAGE_SIZE
    outs_p = _run(x_prompt, sp, 0, zero_conv, zero_ssm, zero_pool, attend_prompt, wts)
    outs_s = _run(x_sample, 1, past_len, state_ssd_conv[0], state_ssd_ssm[0], state_pool[0], attend_sample, wts)
    return (outs_p[0], outs_s[0]) + outs_p[1:] + outs_s[1:]
```
